```python
import jax, jax.numpy as jnp
from jax import lax
import numpy as np

D_MODEL = 1024
BATCH = 1
SEQ = 16384
DEPTH = 2
DEC_BATCH = 8
DEC_SEQ = 64
PAST_LEN = 1024

CHUNK = 64
HEAD_DIM = 64
A_Q_HEADS = 8
A_KV_HEADS = 2
A_GROUP = A_Q_HEADS // A_KV_HEADS
A_WINDOW = 128
A_PREV_CHUNKS = A_WINDOW // CHUNK
ROPE_THETA = 500000.0
ROPE_DIM = HEAD_DIM // 4
B_HEADS = 8
B_PREV_CHUNKS = 8
B_REACH = B_PREV_CHUNKS * CHUNK
REL_CLIP = 128
C_HEADS = 4
C_KEY_DIM = 128
C_VAL_DIM = 128
REC_BLOCK = 16
D_FF = 2816
CONV_W = 3
NORM_EPS = 1e-6
NEG_INF = -1e30

A_Q_W = A_Q_HEADS * HEAD_DIM
A_KV_W = A_KV_HEADS * HEAD_DIM
B_W = B_HEADS * HEAD_DIM
C_K_W = C_HEADS * C_KEY_DIM
C_V_W = C_HEADS * C_VAL_DIM
IN_SPLITS = (A_Q_W, A_KV_W, A_KV_W, B_W, B_W, B_W, C_K_W, C_V_W, C_K_W, C_V_W, D_MODEL, D_MODEL, D_MODEL)
IN_W = sum(IN_SPLITS)
IN_OFFSETS = tuple(int(o) for o in np.cumsum(IN_SPLITS)[:-1])

kernel_name = 'hybrid_stream_swa_band_hgrn2_convffn_step'


def _rmsnorm(x, g):
    xf = x.astype(jnp.float32)
    y = xf * lax.rsqrt(jnp.mean(xf * xf, axis=-1, keepdims=True) + NORM_EPS)
    return (y * g.astype(jnp.float32)).astype(x.dtype)


def _rope(x, pos):
    half = ROPE_DIM // 2
    inv_freq = ROPE_THETA ** (-jnp.arange(0, ROPE_DIM, 2, dtype=jnp.float32) / ROPE_DIM)
    ang = pos.astype(jnp.float32)[:, None] * inv_freq[None, :]
    cos = jnp.cos(ang)[None, :, None, :]
    sin = jnp.sin(ang)[None, :, None, :]
    xr = x[..., :ROPE_DIM].astype(jnp.float32)
    x1, x2 = xr[..., :half], xr[..., half:]
    rot = jnp.concatenate([x1 * cos - x2 * sin, x2 * cos + x1 * sin], axis=-1)
    return jnp.concatenate([rot.astype(x.dtype), x[..., ROPE_DIM:]], axis=-1)


def _chunk_band(x, prev):
    b, t, h, d = x.shape
    nc = t // CHUNK
    xc = x.reshape(b, nc, CHUNK, h, d)
    xp = jnp.pad(xc, ((0, 0), (prev, 0), (0, 0), (0, 0), (0, 0)))
    band = jnp.stack([xp[:, m:m + nc] for m in range(prev + 1)], axis=2)
    return band.reshape(b, nc, (prev + 1) * CHUNK, h, d)


def _band_valid(nc, prev):
    src = jnp.arange(nc)[:, None] - prev + jnp.arange(prev + 1)[None, :]
    return jnp.repeat(src >= 0, CHUNK, axis=1)


def _softmax_with_sink(s, sink):
    sink_col = jnp.broadcast_to(sink, s.shape[:-1] + (1,))
    p = jax.nn.softmax(jnp.concatenate([s, sink_col], axis=-1), axis=-1)
    return p[..., :-1]


def _rel_bias(table, rel):
    idx = jnp.clip(rel, -REL_CLIP, REL_CLIP) + REL_CLIP
    return table[:, idx].astype(jnp.float32)


def _swa_prompt(q, k, v, sinks):
    b, t = q.shape[:2]
    nc = t // CHUNK
    qc = q.reshape(b, nc, CHUNK, A_KV_HEADS, A_GROUP, HEAD_DIM)
    kb = _chunk_band(k, A_PREV_CHUNKS)
    vb = _chunk_band(v, A_PREV_CHUNKS)
    s = jnp.einsum('bnqhgd,bnkhd->bnhgqk', qc, kb).astype(jnp.float32) * (HEAD_DIM ** -0.5)
    valid = _band_valid(nc, A_PREV_CHUNKS)
    s = jnp.where(valid[None, :, None, None, None, :], s, NEG_INF)
    p = _softmax_with_sink(s, sinks.astype(jnp.float32).reshape(A_KV_HEADS, A_GROUP, 1, 1))
    o = jnp.einsum('bnhgqk,bnkhd->bnqhgd', p.astype(v.dtype), vb)
    return o.reshape(b, t, A_Q_W)


def _swa_step(q, k, v, k_cache, v_cache, sinks):
    b, t = q.shape[:2]
    kk = jnp.concatenate([k_cache.astype(k.dtype), k], axis=1)
    vv = jnp.concatenate([v_cache.astype(v.dtype), v], axis=1)
    qg = q.reshape(b, t, A_KV_HEADS, A_GROUP, HEAD_DIM)
    s = jnp.einsum('bqhgd,bkhd->bhgqk', qg, kk).astype(jnp.float32) * (HEAD_DIM ** -0.5)
    p = _softmax_with_sink(s, sinks.astype(jnp.float32).reshape(A_KV_HEADS, A_GROUP, 1, 1))
    o = jnp.einsum('bhgqk,bkhd->bqhgd', p.astype(v.dtype), vv)
    return o.reshape(b, t, A_Q_W), kk[:, t:], vv[:, t:]


def _band_prompt(q, k, v, rel_table):
    b, t = q.shape[:2]
    nc = t // CHUNK
    width = (B_PREV_CHUNKS + 1) * CHUNK
    qc = q.reshape(b, nc, CHUNK, B_HEADS, HEAD_DIM)
    kb = _chunk_band(k, B_PREV_CHUNKS)
    vb = _chunk_band(v, B_PREV_CHUNKS)
    rel = B_REACH + jnp.arange(CHUNK)[:, None] - jnp.arange(width)[None, :]
    bias = _rel_bias(rel_table, rel)
    s = jnp.einsum('bnqhd,bnkhd->bnhqk', qc, kb).astype(jnp.float32) * (HEAD_DIM ** -0.5) + bias[None, None]
    valid = _band_valid(nc, B_PREV_CHUNKS)
    s = jnp.where(valid[None, :, None, None, :], s, NEG_INF)
    p = jax.nn.softmax(s, axis=-1)
    o = jnp.einsum('bnhqk,bnkhd->bnqhd', p.astype(v.dtype), vb)
    return o.reshape(b, t, B_W)


def _band_step(q, k, v, k_cache, v_cache, rel_table):
    b, t = q.shape[:2]
    r = k_cache.shape[1]
    kk = jnp.concatenate([k_cache.astype(k.dtype), k], axis=1)
    vv = jnp.concatenate([v_cache.astype(v.dtype), v], axis=1)
    rel = (r + jnp.arange(t))[:, None] - jnp.arange(r + t)[None, :]
    bias = _rel_bias(rel_table, rel)
    s = jnp.einsum('bqhd,bkhd->bhqk', q, kk).astype(jnp.float32) * (HEAD_DIM ** -0.5) + bias[None]
    p = jax.nn.softmax(s, axis=-1)
    o = jnp.einsum('bhqk,bkhd->bqhd', p.astype(v.dtype), vv)
    return o.reshape(b, t, B_W), kk[:, t:], vv[:, t:]


def _hgrn2_scan(log_f, k, v, q, s0):
    b, t, h, dk = k.shape
    dv = v.shape[-1]
    pad = (-t) % REC_BLOCK
    n = (t + pad) // REC_BLOCK

    def blocks(a):
        a = jnp.pad(a, ((0, 0), (0, pad), (0, 0), (0, 0)))
        return a.reshape(b, n, REC_BLOCK, h, a.shape[-1]).transpose(1, 0, 3, 2, 4)

    lf, kk, vv, qq = blocks(log_f), blocks(k), blocks(v), blocks(q)
    cum = jnp.cumsum(lf, axis=3)
    last = cum[:, :, :, -1:, :]
    causal = jnp.tril(jnp.ones((REC_BLOCK, REC_BLOCK), dtype=bool))[:, :, None]
    diff = cum[:, :, :, :, None, :] - cum[:, :, :, None, :, :]
    decay = jnp.where(causal, jnp.exp(jnp.where(causal, diff, 0.0)), 0.0)
    scores = jnp.einsum('nbhtk,nbhsk,nbhtsk->nbhts', qq, kk, decay)
    o_intra = jnp.einsum('nbhts,nbhsv->nbhtv', scores, vv)
    q_dec = qq * jnp.exp(cum)
    k_dec = kk * jnp.exp(last - cum)
    blk_decay = jnp.exp(last[:, :, :, 0, :])

    def step(state, xs):
        qd, kd, vb, dec = xs
        o = jnp.einsum('bhtk,bhkv->bhtv', qd, state)
        state = dec[..., None] * state + jnp.einsum('bhtk,bhtv->bhkv', kd, vb)
        return state, o

    s_fin, o_inter = lax.scan(step, s0, (q_dec, k_dec, vv, blk_decay))
    o = (o_intra + o_inter).transpose(1, 0, 3, 2, 4).reshape(b, n * REC_BLOCK, h, dv)
    return o[:, :t], s_fin


def _hgrn2_branch(f_pre, i_in, q_pre, og, lb, c_norm, s0):
    f32 = jnp.float32
    b, t = f_pre.shape[:2]
    shp_k = (b, t, C_HEADS, C_KEY_DIM)
    lbh = lb.astype(f32).reshape(C_HEADS, C_KEY_DIM)
    z = f_pre.astype(f32).reshape(shp_k)
    log_f = jnp.logaddexp(jnp.log(lbh), jnp.log1p(-lbh) + jax.nn.log_sigmoid(z))
    k = -jnp.expm1(log_f)
    q = jax.nn.silu(q_pre.astype(f32)).reshape(shp_k)
    v = i_in.astype(f32).reshape(b, t, C_HEADS, C_VAL_DIM)
    o, s_fin = _hgrn2_scan(log_f, k, v, q, s0.astype(f32))
    o = o * lax.rsqrt(jnp.mean(o * o, axis=-1, keepdims=True) + NORM_EPS)
    o = o * c_norm.astype(f32).reshape(C_HEADS, C_VAL_DIM)
    o = o.reshape(b, t, C_V_W) * jax.nn.silu(og.astype(f32))
    return o.astype(f_pre.dtype), s_fin


def _layer(x, pos, lw, cache):
    (norm1, w_in, sinks, rel_table, lb, c_norm, wa, wb, wc, w_out,
     norm2, w_up, w_conv, b_conv, w_down) = lw
    b, t, _ = x.shape
    h = _rmsnorm(x, norm1)
    aq, ak, av, bq, bk, bv, cf, ci, cq, cog, ga, gb, gc = jnp.split(h @ w_in, IN_OFFSETS, axis=-1)
    aq = _rope(aq.reshape(b, t, A_Q_HEADS, HEAD_DIM), pos)
    ak = _rope(ak.reshape(b, t, A_KV_HEADS, HEAD_DIM), pos)
    av = av.reshape(b, t, A_KV_HEADS, HEAD_DIM)
    bq = bq.reshape(b, t, B_HEADS, HEAD_DIM)
    bk = bk.reshape(b, t, B_HEADS, HEAD_DIM)
    bv = bv.reshape(b, t, B_HEADS, HEAD_DIM)
    if cache is None:
        oa = _swa_prompt(aq, ak, av, sinks)
        ra = min(A_WINDOW, t)
        na_k, na_v = ak[:, t - ra:], av[:, t - ra:]
        ob = _band_prompt(bq, bk, bv, rel_table)
        rb = min(B_REACH, t)
        nb_k, nb_v = bk[:, t - rb:], bv[:, t - rb:]
        s0 = jnp.zeros((b, C_HEADS, C_KEY_DIM, C_VAL_DIM), jnp.float32)
        conv_prev = jnp.zeros((b, CONV_W - 1, 2 * D_FF), x.dtype)
    else:
        ca_k, ca_v, cb_k, cb_v, s0, conv_prev = cache
        oa, na_k, na_v = _swa_step(aq, ak, av, ca_k, ca_v, sinks)
        ob, nb_k, nb_v = _band_step(bq, bk, bv, cb_k, cb_v, rel_table)
    oc, nc_state = _hgrn2_branch(cf, ci, cq, cog, lb, c_norm, s0)
    mix = (jax.nn.sigmoid(ga) * (oa @ wa)
           + jax.nn.sigmoid(gb) * (ob @ wb)
           + jax.nn.sigmoid(gc) * (oc @ wc))
    x = x + mix @ w_out
    h2 = _rmsnorm(x, norm2)
    up = h2 @ w_up
    ext = jnp.concatenate([conv_prev.astype(up.dtype), up], axis=1)
    conv = b_conv
    for j in range(CONV_W):
        conv = conv + ext[:, j:j + t] * w_conv[j]
    u, g = jnp.split(conv, 2, axis=-1)
    x = x + (u * jax.nn.gelu(g)) @ w_down
    new_conv = ext[:, t:]
    return x, (na_k, na_v, nb_k, nb_v, nc_state, new_conv)


def setup_inputs(seed: int = 0) -> dict:
    key = jax.random.key(seed)
    ks = jax.random.split(key, 26)
    f32 = jnp.float32

    def nrm(k, shape, scale):
        return jax.random.normal(k, shape, f32) * scale

    ra = min(A_WINDOW, PAST_LEN)
    rb = min(B_REACH, PAST_LEN)
    return {
        'x_prompt': nrm(ks[0], (BATCH, SEQ, D_MODEL), 1.0),
        'x_sample': nrm(ks[1], (DEC_BATCH, DEC_SEQ, D_MODEL), 1.0),
        'cache_a_k': nrm(ks[2], (DEPTH, DEC_BATCH, ra, A_KV_HEADS, HEAD_DIM), 1.0),
        'cache_a_v': nrm(ks[3], (DEPTH, DEC_BATCH, ra, A_KV_HEADS, HEAD_DIM), 1.0),
        'cache_b_k': nrm(ks[4], (DEPTH, DEC_BATCH, rb, B_HEADS, HEAD_DIM), 1.0),
        'cache_b_v': nrm(ks[5], (DEPTH, DEC_BATCH, rb, B_HEADS, HEAD_DIM), 1.0),
        'state_c': nrm(ks[6], (DEPTH, DEC_BATCH, C_HEADS, C_KEY_DIM, C_VAL_DIM), 0.3),
        'state_ffn_conv': nrm(ks[7], (DEPTH, DEC_BATCH, CONV_W - 1, 2 * D_FF), 1.0),
        'norm1': 1.0 + nrm(ks[8], (DEPTH, D_MODEL), 0.02),
        'w_in': nrm(ks[9], (DEPTH, D_MODEL, IN_W), D_MODEL ** -0.5),
        'a_sinks': nrm(ks[10], (DEPTH, A_Q_HEADS), 0.5),
        'b_rel_bias': nrm(ks[11], (DEPTH, B_HEADS, 2 * REL_CLIP + 1), 0.2),
        'c_lb_logits': nrm(ks[12], (DEPTH, C_K_W), 0.5),
        'c_norm': 1.0 + nrm(ks[13], (DEPTH, C_V_W), 0.02),
        'w_branch_a': nrm(ks[14], (DEPTH, A_Q_W, D_MODEL), A_Q_W ** -0.5),
        'w_branch_b': nrm(ks[15], (DEPTH, B_W, D_MODEL), B_W ** -0.5),
        'w_branch_c': nrm(ks[16], (DEPTH, C_V_W, D_MODEL), C_V_W ** -0.5),
        'w_out': nrm(ks[17], (DEPTH, D_MODEL, D_MODEL), D_MODEL ** -0.5),
        'norm2': 1.0 + nrm(ks[18], (DEPTH, D_MODEL), 0.02),
        'w_up': nrm(ks[19], (DEPTH, D_MODEL, 2 * D_FF), D_MODEL ** -0.5),
        'w_conv': nrm(ks[20], (DEPTH, CONV_W, 2 * D_FF), CONV_W ** -0.5),
        'b_conv': nrm(ks[21], (DEPTH, 2 * D_FF), 0.02),
        'w_down': nrm(ks[22], (DEPTH, D_FF, D_MODEL), D_FF ** -0.5),
        'norm_final': 1.0 + nrm(ks[23], (D_MODEL,), 0.02),
    }


def reference(x_prompt, x_sample, cache_a_k, cache_a_v, cache_b_k, cache_b_v, state_c, state_ffn_conv,
              norm1, w_in, a_sinks, b_rel_bias, c_lb_logits, c_norm, w_branch_a, w_branch_b, w_branch_c,
              w_out, norm2, w_up, w_conv, b_conv, w_down, norm_final):
    lb_cum = jnp.cumsum(jax.nn.softmax(c_lb_logits.astype(jnp.float32), axis=0), axis=0)
    lower_bounds = lb_cum - lb_cum[0:1]
    pos_p = jnp.arange(x_prompt.shape[1], dtype=jnp.int32)
    pos_s = PAST_LEN + jnp.arange(x_sample.shape[1], dtype=jnp.int32)
    xp, xs = x_prompt, x_sample
    st_p, st_s = [], []
    for l in range(DEPTH):
        lw = (norm1[l], w_in[l], a_sinks[l], b_rel_bias[l], lower_bounds[l], c_norm[l],
              w_branch_a[l], w_branch_b[l], w_branch_c[l], w_out[l], norm2[l], w_up[l],
              w_conv[l], b_conv[l], w_down[l])
        xp, sp = _layer(xp, pos_p, lw, None)
        xs, ss = _layer(xs, pos_s, lw, (cache_a_k[l], cache_a_v[l], cache_b_k[l], cache_b_v[l],
                                        state_c[l], state_ffn_conv[l]))
        st_p.append(sp)
        st_s.append(ss)
    y_prompt = _rmsnorm(xp, norm_final)
    y_sample = _rmsnorm(xs, norm_final)

    def stk(sts, i):
        return jnp.stack([s[i] for s in sts], axis=0)

    return (y_prompt, y_sample,
            stk(st_p, 0), stk(st_p, 1), stk(st_p, 2), stk(st_p, 3), stk(st_p, 4), stk(st_p, 5),
            stk(st_s, 0), stk(st_s, 1), stk(st_s, 2), stk(st_s, 3), stk(st_s, 4), stk(st_s, 5))
```

```python
import functools

import numpy as np
import jax
import jax.numpy as jnp
from jax import lax
from jax.experimental import pallas as pl
from jax.experimental.pallas import tpu as pltpu

F32 = jnp.float32
BF16 = jnp.bfloat16

CHUNK = 64
HEAD_DIM = 64
LANES = 128
A_Q_HEADS = 8
A_KV_HEADS = 2
A_GROUP = A_Q_HEADS // A_KV_HEADS
A_PREV_CHUNKS = 2
B_HEADS = 8
B_PREV_CHUNKS = 8
REL_CLIP = 128
ROPE_DIM = HEAD_DIM // 4
ROPE_THETA = 500000.0
C_HEADS = 4
C_DIM = 128
REC_BLOCK = 16
CONV_W = 3
NORM_EPS = 1e-6
NEG_INF = -1e30
PAST_LEN = 1024

ATT_W = A_Q_HEADS * HEAD_DIM
A_KV_W = A_KV_HEADS * HEAD_DIM
C_W = C_HEADS * C_DIM
N_SLABS = ATT_W // LANES

VMEM_LIMIT_BYTES = 56 * 1024 * 1024


def _params(*sem):
    return pltpu.CompilerParams(dimension_semantics=sem, vmem_limit_bytes=VMEM_LIMIT_BYTES)


def _resident(shape):
    nd = len(shape)
    return pl.BlockSpec(shape, lambda *_: (0,) * nd, pipeline_mode=pl.Buffered(1))


def _rmsnorm(x, g):
    return x * lax.rsqrt(jnp.mean(x * x, axis=-1, keepdims=True) + NORM_EPS) * g


def _inproj_kernel(x_ref, g_ref, w_ref, cos_ref, sin_ref,
                   aq_ref, ak_ref, av_ref, bq_ref, bk_ref, bv_ref, c_ref, gate_ref):
    h = _rmsnorm(x_ref[...], g_ref[...]).astype(BF16)
    cos = cos_ref[...]
    sin = sin_ref[...]
    lane = lax.broadcasted_iota(jnp.int32, cos.shape, 1)
    first_half = (lane % HEAD_DIM) < (ROPE_DIM // 2)

    def proj(lo, hi):
        return jnp.dot(h, w_ref[:, lo:hi], preferred_element_type=F32)

    def rope(p):
        rot = jnp.where(first_half,
                        -pltpu.roll(p, LANES - ROPE_DIM // 2, 1),
                        pltpu.roll(p, ROPE_DIM // 2, 1))
        return p * cos + rot * sin

    o = 0
    p = proj(o, o + ATT_W)
    for s in range(N_SLABS):
        aq_ref[:, s * LANES:(s + 1) * LANES] = rope(p[:, s * LANES:(s + 1) * LANES]).astype(aq_ref.dtype)
    o += ATT_W
    p = proj(o, o + 2 * A_KV_W)
    ak_ref[...] = rope(p[:, :A_KV_W]).astype(ak_ref.dtype)
    av_ref[...] = p[:, A_KV_W:].astype(av_ref.dtype)
    o += 2 * A_KV_W
    for ref in (bq_ref, bk_ref, bv_ref):
        ref[...] = proj(o, o + ATT_W).astype(ref.dtype)
        o += ATT_W
    for s in range(4):
        c_ref[:, s * C_W:(s + 1) * C_W] = proj(o, o + C_W).astype(c_ref.dtype)
        o += C_W
    d_model = x_ref.shape[1]
    for s in range(3):
        gate_ref[:, s * d_model:(s + 1) * d_model] = proj(o, o + d_model).astype(gate_ref.dtype)
        o += d_model


def _inproj(x, g, w, cos, sin, tm):
    rows, d_model = x.shape
    in_w = w.shape[1]
    widths = (ATT_W, A_KV_W, A_KV_W, ATT_W, ATT_W, ATT_W, 4 * C_W, 3 * d_model)
    assert sum(widths) == in_w and rows % tm == 0
    row_block = lambda width: pl.BlockSpec((tm, width), lambda i: (i, 0))
    return pl.pallas_call(
        _inproj_kernel,
        grid=(rows // tm,),
        in_specs=[row_block(d_model), _resident((1, d_model)), _resident((d_model, in_w)),
                  row_block(LANES), row_block(LANES)],
        out_specs=[row_block(wd) for wd in widths],
        out_shape=[jax.ShapeDtypeStruct((rows, wd), BF16) for wd in widths],
        compiler_params=_params("arbitrary"),
        name="inproj",
    )(x, g, w, cos, sin)


def _attn_kernel(*refs, nq, prev, k_slabs, use_bias, use_sink, mask_first, aliased):
    refs = list(refs)
    if aliased:
        refs.pop(0)
    q_ref, kp_ref, kc_ref, vp_ref, vc_ref, extra_ref, o_ref, kf_ref, vf_ref = refs
    hist = prev * CHUNK
    win = (prev + 1) * CHUNK
    kf_ref[0:hist, :] = kp_ref[...].astype(BF16)
    kf_ref[hist:hist + nq * CHUNK, :] = kc_ref[...]
    vf_ref[0:hist, :] = vp_ref[...].astype(BF16)
    vf_ref[hist:hist + nq * CHUNK, :] = vc_ref[...]

    lower_head = lax.broadcasted_iota(jnp.int32, (CHUNK, LANES), 1) < HEAD_DIM
    col = lax.broadcasted_iota(jnp.int32, (2 * CHUNK, win), 1)
    is_first = pl.program_id(0) == 0

    for j in range(nq):
        rows = slice(j * CHUNK, (j + 1) * CHUNK)
        for s in range(N_SLABS):
            lanes = slice(s * LANES, (s + 1) * LANES)
            ks = s if k_slabs == N_SLABS else 0
            klanes = slice(ks * LANES, (ks + 1) * LANES)
            q2 = q_ref[rows, lanes]
            zero = jnp.zeros_like(q2)
            qq = jnp.concatenate([jnp.where(lower_head, q2, zero),
                                  jnp.where(lower_head, zero, q2)], axis=0)
            kw = kf_ref[j * CHUNK:j * CHUNK + win, klanes]
            vw = vf_ref[j * CHUNK:j * CHUNK + win, klanes]
            sc = lax.dot_general(qq, kw, (((1,), (1,)), ((), ())), preferred_element_type=F32)
            if use_bias:
                sc = sc + extra_ref[s]
            if mask_first and j < prev:
                n_invalid = jnp.where(is_first, (prev - j) * CHUNK, 0)
                sc = jnp.where(col < n_invalid, NEG_INF, sc)
            m = jnp.max(sc, axis=-1, keepdims=True)
            if use_sink:
                sink = extra_ref[s]
                m = jnp.maximum(m, sink)
            e = jnp.exp(sc - m)
            den = jnp.sum(e, axis=-1, keepdims=True)
            if use_sink:
                den = den + jnp.exp(sink - m)
            o2 = jnp.dot(e.astype(BF16), vw, preferred_element_type=F32) / den
            o_ref[rows, lanes] = jnp.where(lower_head, o2[:CHUNK], o2[CHUNK:]).astype(o_ref.dtype)


def _attention(q, k, v, extra, *, prev, use_bias, use_sink, nq, n_steps, q_block0,
               k_prev=None, v_prev=None, out=None, out_rows=None):
    kw = k.shape[1]
    hist = prev * CHUNK
    tq = nq * CHUNK
    from_cache = k_prev is not None
    cur = lambda width: pl.BlockSpec((tq, width), lambda i: (q_block0 + i, 0))
    if from_cache:
        prev_spec = pl.BlockSpec((hist, kw), lambda i: (i, 0))
        kp, vp = k_prev, v_prev
    else:
        assert tq % hist == 0 and q_block0 == 0
        ratio = tq // hist
        prev_spec = pl.BlockSpec((hist, kw), lambda i: (jnp.maximum(i * ratio - 1, 0), 0))
        kp, vp = k, v
    extra_spec = _resident(extra.shape)
    in_specs = [cur(ATT_W), prev_spec, cur(kw), prev_spec, cur(kw), extra_spec]
    args = [q, kp, k, vp, v, extra]
    aliases = {}
    if out is not None:
        in_specs = [pl.BlockSpec(memory_space=pl.ANY)] + in_specs
        args = [out] + args
        aliases = {0: 0}
        out_rows = out.shape[0]
    kern = functools.partial(_attn_kernel, nq=nq, prev=prev, k_slabs=kw // LANES, use_bias=use_bias,
                             use_sink=use_sink, mask_first=not from_cache, aliased=out is not None)
    return pl.pallas_call(
        kern,
        grid=(n_steps,),
        in_specs=in_specs,
        out_specs=cur(ATT_W),
        out_shape=jax.ShapeDtypeStruct((out_rows, ATT_W), BF16),
        scratch_shapes=[pltpu.VMEM((hist + tq, kw), BF16), pltpu.VMEM((hist + tq, kw), BF16)],
        input_output_aliases=aliases,
        compiler_params=_params("arbitrary"),
        name="attn_b" if use_bias else "attn_a",
    )(*args)


def _hgrn_kernel(*refs, layer, depth, aliased):
    refs = list(refs)
    if aliased:
        refs.pop(0)
    (c_ref, lbl_ref, cn_ref, s0_ref, o_ref, sfin_ref,
     st_ref, qd_ref, kd_ref, v_ref, dec_ref, oi_ref) = refs
    tm = c_ref.shape[0]
    nblk = tm // REC_BLOCK
    t = pl.program_id(1)

    @pl.when(t == 0)
    def _():
        for hd in range(C_HEADS):
            st_ref[hd] = s0_ref[0, hd].T

    logit_rows = [lbl_ref[i:i + 1, :] for i in range(depth)]
    mx = functools.reduce(jnp.maximum, logit_rows)
    ex = [jnp.exp(r - mx) for r in logit_rows]
    den = functools.reduce(jnp.add, ex)
    sm = [e / den for e in ex]
    lb_all = functools.reduce(jnp.add, sm[:layer + 1]) - sm[0]
    log_lb = jnp.log(lb_all)
    log_1mlb = jnp.log1p(-lb_all)

    pos = lax.broadcasted_iota(jnp.int32, (tm, C_DIM), 0) % REC_BLOCK

    o_intra = []
    og_all = []
    for hd in range(C_HEADS):
        lanes = slice(hd * C_DIM, (hd + 1) * C_DIM)
        z = c_ref[:, lanes].astype(F32)
        vv = c_ref[:, C_W + hd * C_DIM:C_W + (hd + 1) * C_DIM].astype(F32)
        qp = c_ref[:, 2 * C_W + hd * C_DIM:2 * C_W + (hd + 1) * C_DIM].astype(F32)
        q = qp * jax.nn.sigmoid(qp)
        log_sig = -(jnp.maximum(-z, 0.0) + jnp.log1p(jnp.exp(-jnp.abs(z))))
        a = log_lb[:, lanes]
        b = log_1mlb[:, lanes] + log_sig
        lf = jnp.maximum(a, b) + jnp.log1p(jnp.exp(-jnp.abs(a - b)))
        kk = 1.0 - jnp.exp(lf)
        cum = lf
        for sft in (1, 2, 4, 8):
            cum = cum + jnp.where(pos >= sft, pltpu.roll(cum, sft, 0), 0.0)
        last = jnp.where(pos == REC_BLOCK - 1, cum, 0.0)
        for sft in (1, 2, 4, 8):
            last = last + jnp.where(pos < REC_BLOCK - sft, pltpu.roll(last, tm - sft, 0), 0.0)
        qd_ref[hd] = (q * jnp.exp(cum)).astype(BF16)
        kd_ref[hd] = (kk * jnp.exp(last - cum)).astype(BF16)
        v_ref[hd] = vv.astype(BF16)
        dec_ref[hd] = jnp.exp(last)
        acc = jnp.zeros((tm, C_DIM), F32)
        for d in range(REC_BLOCK):
            kr = kk if d == 0 else pltpu.roll(kk, d, 0)
            cr = cum if d == 0 else pltpu.roll(cum, d, 0)
            vr = vv if d == 0 else pltpu.roll(vv, d, 0)
            w = q * kr * jnp.exp(jnp.minimum(cum - cr, 0.0))
            score = jnp.sum(w, axis=-1, keepdims=True)
            acc = acc + jnp.where(pos >= d, score, 0.0) * vr
        o_intra.append(acc)
        og_all.append(c_ref[:, 3 * C_W + hd * C_DIM:3 * C_W + (hd + 1) * C_DIM].astype(F32))

    def block_step(n, carry):
        r0 = pl.multiple_of(n * REC_BLOCK, REC_BLOCK)
        for hd in range(C_HEADS):
            st = st_ref[hd]
            qd = qd_ref[hd, pl.ds(r0, REC_BLOCK), :]
            kd = kd_ref[hd, pl.ds(r0, REC_BLOCK), :]
            vb = v_ref[hd, pl.ds(r0, REC_BLOCK), :]
            oi_ref[hd, pl.ds(r0, REC_BLOCK), :] = lax.dot_general(
                qd, st.astype(BF16), (((1,), (1,)), ((), ())), preferred_element_type=F32)
            upd = lax.dot_general(vb, kd, (((0,), (0,)), ((), ())), preferred_element_type=F32)
            dec = dec_ref[hd, pl.ds(r0, 8), :]
            st_ref[hd] = st * jnp.concatenate([dec] * (C_DIM // 8), axis=0) + upd
        return carry

    lax.fori_loop(0, nblk, block_step, 0)

    for hd in range(C_HEADS):
        lanes = slice(hd * C_DIM, (hd + 1) * C_DIM)
        o = o_intra[hd] + oi_ref[hd]
        o = o * lax.rsqrt(jnp.mean(o * o, axis=-1, keepdims=True) + NORM_EPS) * cn_ref[:, lanes]
        og = og_all[hd]
        o_ref[:, lanes] = (o * (og * jax.nn.sigmoid(og))).astype(o_ref.dtype)

    @pl.when(t == pl.num_programs(1) - 1)
    def _():
        for hd in range(C_HEADS):
            sfin_ref[0, hd] = st_ref[hd].T


def _hgrn(c, lb_logits, c_norm, s0, *, layer, tm, n_batch, n_tiles, row_block0, out=None, out_rows=None):
    depth = lb_logits.shape[0]
    rows_spec = lambda width: pl.BlockSpec((tm, width), lambda b, t: (row_block0 + b * n_tiles + t, 0))
    state_spec = pl.BlockSpec((1, C_HEADS, C_DIM, C_DIM), lambda b, t: (b, 0, 0, 0))
    in_specs = [rows_spec(4 * C_W), _resident(lb_logits.shape), _resident(c_norm.shape), state_spec]
    args = [c, lb_logits, c_norm, s0]
    aliases = {}
    if out is not None:
        in_specs = [pl.BlockSpec(memory_space=pl.ANY)] + in_specs
        args = [out] + args
        aliases = {0: 0}
        out_rows = out.shape[0]
    kern = functools.partial(_hgrn_kernel, layer=layer, depth=depth, aliased=out is not None)
    return pl.pallas_call(
        kern,
        grid=(n_batch, n_tiles),
        in_specs=in_specs,
        out_specs=[rows_spec(C_W), state_spec],
        out_shape=[jax.ShapeDtypeStruct((out_rows, C_W), BF16),
                   jax.ShapeDtypeStruct((n_batch, C_HEADS, C_DIM, C_DIM), F32)],
        scratch_shapes=[pltpu.VMEM((C_HEADS, C_DIM, C_DIM), F32),
                        pltpu.VMEM((C_HEADS, tm, C_DIM), BF16),
                        pltpu.VMEM((C_HEADS, tm, C_DIM), BF16),
                        pltpu.VMEM((C_HEADS, tm, C_DIM), BF16),
                        pltpu.VMEM((C_HEADS, tm, C_DIM), F32),
                        pltpu.VMEM((C_HEADS, tm, C_DIM), F32)],
        input_output_aliases=aliases,
        compiler_params=_params("arbitrary", "arbitrary"),
        name="hgrn2",
    )(*args)


def _merge_kernel(x_ref, oa_ref, ob_ref, oc_ref, ga_ref, gb_ref, gc_ref,
                  wa_ref, wb_ref, wc_ref, wo_ref, y_ref):
    def branch(o_ref, w_ref, g_ref):
        return jax.nn.sigmoid(g_ref[...].astype(F32)) * jnp.dot(o_ref[...], w_ref[...], preferred_element_type=F32)

    mix = branch(oa_ref, wa_ref, ga_ref) + branch(ob_ref, wb_ref, gb_ref) + branch(oc_ref, wc_ref, gc_ref)
    y_ref[...] = x_ref[...] + jnp.dot(mix.astype(BF16), wo_ref[...], preferred_element_type=F32)


def _merge(x, oa, ob, oc, gates, wa, wb, wc, wo, tm):
    rows, d_model = x.shape
    row_block = lambda width, j=0: pl.BlockSpec((tm, width), lambda i: (i, j))
    return pl.pallas_call(
        _merge_kernel,
        grid=(rows // tm,),
        in_specs=[row_block(d_model), row_block(ATT_W), row_block(ATT_W), row_block(C_W),
                  row_block(d_model, 0), row_block(d_model, 1), row_block(d_model, 2),
                  _resident(wa.shape), _resident(wb.shape), _resident(wc.shape), _resident(wo.shape)],
        out_specs=row_block(d_model),
        out_shape=jax.ShapeDtypeStruct((rows, d_model), F32),
        compiler_params=_params("arbitrary"),
        name="merge",
    )(x, oa, ob, oc, gates, gates, gates, wa, wb, wc, wo)


FFN_COLS = 256
EXT_PAD = 8


def _ffn_kernel(*refs, n_seg, seg_len, d_ff, use_state, final_norm, aliased):
    refs = list(refs)
    if aliased:
        refs.pop(0)
    x_ref, g2_ref, wup_ref, wconv_ref, bconv_ref, wdown_ref = refs[:6]
    refs = refs[6:]
    state_ref = refs.pop(0) if use_state else None
    gfin_ref = refs.pop(0) if final_norm else None
    y_ref, newconv_ref, ext_ref, carry_ref = refs
    cw = FFN_COLS
    tm = n_seg * seg_len
    i = pl.program_id(0)

    if not use_state:
        @pl.when(i == 0)
        def _():
            carry_ref[...] = jnp.zeros_like(carry_ref)

    x = x_ref[...]
    h = _rmsnorm(x, g2_ref[...]).astype(BF16)
    acc = jnp.zeros((tm, x.shape[1]), F32)
    for c in range(d_ff // cw):
        halves = []
        for col0 in (c * cw, d_ff + c * cw):
            cols = slice(col0, col0 + cw)
            up = jnp.dot(h, wup_ref[:, cols], preferred_element_type=F32)
            w0 = wconv_ref[0:1, cols]
            w1 = wconv_ref[1:2, cols]
            w2 = wconv_ref[2:3, cols]
            bias = bconv_ref[:, cols]
            segs = []
            for sg in range(n_seg):
                up_s = up[sg * seg_len:(sg + 1) * seg_len]
                if use_state:
                    before = state_ref[sg, :, cols]
                else:
                    before = carry_ref[EXT_PAD - 2:EXT_PAD, cols]
                ext_ref[sg, EXT_PAD - 2:EXT_PAD, :] = before
                ext_ref[sg, EXT_PAD:EXT_PAD + seg_len, :] = up_s
                conv = (bias + w0 * ext_ref[sg, EXT_PAD - 2:EXT_PAD - 2 + seg_len, :]
                        + w1 * ext_ref[sg, EXT_PAD - 1:EXT_PAD - 1 + seg_len, :] + w2 * up_s)
                segs.append(conv)
                tail = up_s[seg_len - 2:seg_len]
                if use_state:
                    newconv_ref[sg, :, cols] = tail
                else:
                    carry_ref[EXT_PAD - 2:EXT_PAD, cols] = tail
                    newconv_ref[0, :, cols] = tail
            halves.append(segs[0] if n_seg == 1 else jnp.concatenate(segs, axis=0))
        act = (halves[0] * jax.nn.gelu(halves[1])).astype(BF16)
        acc = acc + jnp.dot(act, wdown_ref[c * cw:(c + 1) * cw, :], preferred_element_type=F32)
    y = x + acc
    if final_norm:
        y = _rmsnorm(y, gfin_ref[...])
    y_ref[...] = y


def _ffn(x, g2, w_up, w_conv, b_conv, w_down, *, n_seg, seg_len, n_steps, row_block0,
         state=None, g_final=None, out=None, out_rows=None, out_block0=None):
    d_model = x.shape[1]
    d_ff = w_down.shape[0]
    assert d_ff % FFN_COLS == 0
    tm = n_seg * seg_len
    use_state = state is not None
    final_norm = g_final is not None
    out_block0 = row_block0 if out_block0 is None else out_block0
    in_specs = [pl.BlockSpec((tm, d_model), lambda i: (row_block0 + i, 0)),
                _resident(g2.shape), _resident(w_up.shape), _resident(w_conv.shape),
                _resident(b_conv.shape), _resident(w_down.shape)]
    args = [x, g2, w_up, w_conv, b_conv, w_down]
    if use_state:
        assert n_steps == 1
        in_specs.append(_resident(state.shape))
        args.append(state)
    if final_norm:
        in_specs.append(_resident(g_final.shape))
        args.append(g_final)
    aliases = {}
    if out is not None:
        in_specs = [pl.BlockSpec(memory_space=pl.ANY)] + in_specs
        args = [out] + args
        aliases = {0: 0}
        out_rows = out.shape[0]
    n_state = n_seg if use_state else 1
    kern = functools.partial(_ffn_kernel, n_seg=n_seg, seg_len=seg_len, d_ff=d_ff, use_state=use_state,
                             final_norm=final_norm, aliased=out is not None)
    return pl.pallas_call(
        kern,
        grid=(n_steps,),
        in_specs=in_specs,
        out_specs=[pl.BlockSpec((tm, d_model), lambda i: (out_block0 + i, 0)),
                   pl.BlockSpec((n_state, CONV_W - 1, 2 * d_ff), lambda i: (0, 0, 0))],
        out_shape=[jax.ShapeDtypeStruct((out_rows, d_model), F32),
                   jax.ShapeDtypeStruct((n_state, CONV_W - 1, 2 * d_ff), F32)],
        scratch_shapes=[pltpu.VMEM((n_seg, EXT_PAD + seg_len, FFN_COLS), F32),
                        pltpu.VMEM((EXT_PAD, 2 * d_ff), F32)],
        input_output_aliases=aliases,
        compiler_params=_params("arbitrary"),
        name="convffn",
    )(*args)


A_HEAD_ORDER = tuple(h for s in range(A_GROUP) for h in (s, A_GROUP + s))


def _rope_tables(pos):
    half = ROPE_DIM // 2
    inv_freq = ROPE_THETA ** (-jnp.arange(0, ROPE_DIM, 2, dtype=F32) / ROPE_DIM)
    ang = pos.astype(F32)[:, None] * inv_freq[None, :]
    d = jnp.arange(LANES) % HEAD_DIM
    rotated = d < ROPE_DIM
    cos = jnp.where(rotated[None, :], jnp.cos(ang)[:, d % half], 1.0)
    sin = jnp.where(rotated[None, :], jnp.sin(ang)[:, d % half], 0.0)
    return cos, sin


def _prep_w_in(w_in_l):
    d_model = w_in_l.shape[0]
    scale = HEAD_DIM ** -0.5
    aq = w_in_l[:, :ATT_W].reshape(d_model, A_Q_HEADS, HEAD_DIM)[:, A_HEAD_ORDER, :].reshape(d_model, ATT_W)
    b0 = ATT_W + 2 * A_KV_W
    return jnp.concatenate([aq * scale, w_in_l[:, ATT_W:b0], w_in_l[:, b0:b0 + ATT_W] * scale,
                            w_in_l[:, b0 + ATT_W:]], axis=1).astype(BF16)


def _rel_bias_pairs(table):
    width = (B_PREV_CHUNKS + 1) * CHUNK
    rel = B_PREV_CHUNKS * CHUNK + jnp.arange(CHUNK)[:, None] - jnp.arange(width)[None, :]
    idx = jnp.clip(rel, -REL_CLIP, REL_CLIP) + REL_CLIP
    return table[:, idx].astype(F32).reshape(B_HEADS // 2, 2 * CHUNK, width)


def _sink_pairs(sinks):
    s = sinks.astype(F32)[jnp.array(A_HEAD_ORDER)]
    return jnp.repeat(s, CHUNK).reshape(N_SLABS, 2 * CHUNK, 1)


def kernel(x_prompt, x_sample, cache_a_k, cache_a_v, cache_b_k, cache_b_v, state_c, state_ffn_conv,
           norm1, w_in, a_sinks, b_rel_bias, c_lb_logits, c_norm, w_branch_a, w_branch_b, w_branch_c,
           w_out, norm2, w_up, w_conv, b_conv, w_down, norm_final):
    batch, seq, d_model = x_prompt.shape
    dec_batch, dec_seq, _ = x_sample.shape
    depth = w_in.shape[0]
    d_ff = w_down.shape[1]
    ra = cache_a_k.shape[2]
    rb = cache_b_k.shape[2]
    tm = 512
    assert batch == 1 and dec_seq == CHUNK and seq % tm == 0 and (dec_batch * dec_seq) % tm == 0
    assert ra == A_PREV_CHUNKS * CHUNK and rb == B_PREV_CHUNKS * CHUNK
    n_prompt = batch * seq
    n_sample = dec_batch * dec_seq
    rows = n_prompt + n_sample
    p_tiles = n_prompt // tm

    x = jnp.concatenate([x_prompt.reshape(n_prompt, d_model), x_sample.reshape(n_sample, d_model)], axis=0)
    pos = jnp.concatenate([jnp.arange(seq, dtype=jnp.int32),
                           jnp.tile(PAST_LEN + jnp.arange(dec_seq, dtype=jnp.int32), dec_batch)])
    cos, sin = _rope_tables(pos)
    lb_logits = c_lb_logits.astype(F32)
    zero_state = jnp.zeros((batch, C_HEADS, C_DIM, C_DIM), F32)

    prompt_states = []
    sample_states = []
    y_prompt = y_sample = None
    for l in range(depth):
        wa = w_branch_a[l].reshape(A_Q_HEADS, HEAD_DIM, d_model)[A_HEAD_ORDER, :, :].reshape(ATT_W, d_model)
        aq, ak, av, bq, bk, bv, c, gates = _inproj(
            x, norm1[l].reshape(1, d_model), _prep_w_in(w_in[l]), cos, sin, tm=256)

        sinks = _sink_pairs(a_sinks[l])
        nq = tm // CHUNK
        oa = _attention(aq, ak, av, sinks, prev=A_PREV_CHUNKS, use_bias=False, use_sink=True,
                        nq=nq, n_steps=p_tiles, q_block0=0, out_rows=rows)
        oa = _attention(aq, ak, av, sinks, prev=A_PREV_CHUNKS, use_bias=False, use_sink=True,
                        nq=1, n_steps=dec_batch, q_block0=n_prompt // CHUNK,
                        k_prev=cache_a_k[l].reshape(dec_batch * ra, A_KV_W),
                        v_prev=cache_a_v[l].reshape(dec_batch * ra, A_KV_W), out=oa)
        bias = _rel_bias_pairs(b_rel_bias[l])
        ob = _attention(bq, bk, bv, bias, prev=B_PREV_CHUNKS, use_bias=True, use_sink=False,
                        nq=nq, n_steps=p_tiles, q_block0=0, out_rows=rows)
        ob = _attention(bq, bk, bv, bias, prev=B_PREV_CHUNKS, use_bias=True, use_sink=False,
                        nq=1, n_steps=dec_batch, q_block0=n_prompt // CHUNK,
                        k_prev=cache_b_k[l].reshape(dec_batch * rb, ATT_W),
                        v_prev=cache_b_v[l].reshape(dec_batch * rb, ATT_W), out=ob)
        cn = c_norm[l].reshape(1, C_W).astype(F32)
        oc, c_state_p = _hgrn(c, lb_logits, cn, zero_state, layer=l, tm=tm, n_batch=batch,
                              n_tiles=p_tiles, row_block0=0, out_rows=rows)
        oc, c_state_s = _hgrn(c, lb_logits, cn, state_c[l].astype(F32), layer=l, tm=dec_seq, n_batch=dec_batch,
                              n_tiles=1, row_block0=n_prompt // dec_seq, out=oc)

        x1 = _merge(x, oa, ob, oc, gates, wa.astype(BF16), w_branch_b[l].astype(BF16),
                    w_branch_c[l].astype(BF16), w_out[l].astype(BF16), tm=tm)

        last = l == depth - 1
        g_final = norm_final.reshape(1, d_model) if last else None
        ffn_w = (norm2[l].reshape(1, d_model), w_up[l].astype(BF16), w_conv[l], b_conv[l].reshape(1, 2 * d_ff),
                 w_down[l].astype(BF16))
        if last:
            y_prompt, conv_p = _ffn(x1, *ffn_w, n_seg=1, seg_len=tm, n_steps=p_tiles, row_block0=0,
                                    g_final=g_final, out_rows=n_prompt)
            y_sample, conv_s = _ffn(x1, *ffn_w, n_seg=dec_batch, seg_len=dec_seq, n_steps=1, row_block0=p_tiles,
                                    state=state_ffn_conv[l], g_final=g_final, out_rows=n_sample, out_block0=0)
        else:
            x2, conv_p = _ffn(x1, *ffn_w, n_seg=1, seg_len=tm, n_steps=p_tiles, row_block0=0, out_rows=rows)
            x, conv_s = _ffn(x1, *ffn_w, n_seg=dec_batch, seg_len=dec_seq, n_steps=1, row_block0=p_tiles,
                             state=state_ffn_conv[l], out=x2)

        ak_s = ak[n_prompt:].astype(F32).reshape(dec_batch, dec_seq, A_KV_HEADS, HEAD_DIM)
        av_s = av[n_prompt:].astype(F32).reshape(dec_batch, dec_seq, A_KV_HEADS, HEAD_DIM)
        bk_s = bk[n_prompt:].astype(F32).reshape(dec_batch, dec_seq, B_HEADS, HEAD_DIM)
        bv_s = bv[n_prompt:].astype(F32).reshape(dec_batch, dec_seq, B_HEADS, HEAD_DIM)
        prompt_states.append((
            ak[n_prompt - ra:n_prompt].astype(F32).reshape(batch, ra, A_KV_HEADS, HEAD_DIM),
            av[n_prompt - ra:n_prompt].astype(F32).reshape(batch, ra, A_KV_HEADS, HEAD_DIM),
            bk[n_prompt - rb:n_prompt].astype(F32).reshape(batch, rb, B_HEADS, HEAD_DIM),
            bv[n_prompt - rb:n_prompt].astype(F32).reshape(batch, rb, B_HEADS, HEAD_DIM),
            c_state_p, conv_p))
        sample_states.append((
            jnp.concatenate([cache_a_k[l], ak_s], axis=1)[:, dec_seq:],
            jnp.concatenate([cache_a_v[l], av_s], axis=1)[:, dec_seq:],
            jnp.concatenate([cache_b_k[l], bk_s], axis=1)[:, dec_seq:],
            jnp.concatenate([cache_b_v[l], bv_s], axis=1)[:, dec_seq:],
            c_state_s, conv_s))

    def stk(sts, i):
        return jnp.stack([s[i] for s in sts], axis=0)

    return (y_prompt.reshape(batch, seq, d_model), y_sample.reshape(dec_batch, dec_seq, d_model),
            stk(prompt_states, 0), stk(prompt_states, 1), stk(prompt_states, 2), stk(prompt_states, 3),
            stk(prompt_states, 4), stk(prompt_states, 5),
            stk(sample_states, 0), stk(sample_states, 1), stk(sample_states, 2), stk(sample_states, 3),
            stk(sample_states, 4), stk(sample_states, 5))
```

```python
import functools

import numpy as np
import jax
import jax.numpy as jnp
from jax import lax
from jax.experimental import pallas as pl
from jax.experimental.pallas import tpu as pltpu

F32 = jnp.float32
BF16 = jnp.bfloat16

CHUNK = 64
HEAD_DIM = 64
LANES = 128
A_Q_HEADS = 8
A_KV_HEADS = 2
A_GROUP = A_Q_HEADS // A_KV_HEADS
A_PREV_CHUNKS = 2
B_HEADS = 8
B_PREV_CHUNKS = 8
REL_CLIP = 128
ROPE_DIM = HEAD_DIM // 4
ROPE_THETA = 500000.0
C_HEADS = 4
C_DIM = 128
REC_BLOCK = 16
CONV_W = 3
NORM_EPS = 1e-6
NEG_INF = -1e30
PAST_LEN = 1024

ATT_W = A_Q_HEADS * HEAD_DIM
A_KV_W = A_KV_HEADS * HEAD_DIM
C_W = C_HEADS * C_DIM
N_SLABS = ATT_W // LANES

VMEM_LIMIT_BYTES = 56 * 1024 * 1024


def _params(*sem):
    return pltpu.CompilerParams(dimension_semantics=sem, vmem_limit_bytes=VMEM_LIMIT_BYTES)


def _resident(shape):
    nd = len(shape)
    return pl.BlockSpec(shape, lambda *_: (0,) * nd, pipeline_mode=pl.Buffered(1))


def _rmsnorm(x, g):
    return x * lax.rsqrt(jnp.mean(x * x, axis=-1, keepdims=True) + NORM_EPS) * g


def _stacked_rows(x_refs, first_tiles):
    if len(x_refs) == 1:
        return x_refs[0][...]
    return jnp.where(pl.program_id(0) < first_tiles, x_refs[0][...], x_refs[1][...])


def _stacked_specs(xs, tm):
    if len(xs) == 1:
        return [pl.BlockSpec((tm, xs[0].shape[1]), lambda i: (i, 0))], 0
    first_tiles = xs[0].shape[0] // tm
    assert xs[0].shape[0] % tm == 0 and xs[1].shape[0] % tm == 0
    return [pl.BlockSpec((tm, xs[0].shape[1]), lambda i: (jnp.minimum(i, first_tiles - 1), 0)),
            pl.BlockSpec((tm, xs[1].shape[1]), lambda i: (jnp.maximum(i - first_tiles, 0), 0))], first_tiles


def _inproj_kernel(*refs, n_x, first_tiles):
    x_refs = refs[:n_x]
    (g_ref, w_ref, cos_ref, sin_ref,
     aq_ref, ak_ref, av_ref, bq_ref, bk_ref, bv_ref, c_ref, gate_ref) = refs[n_x:]
    h = _rmsnorm(_stacked_rows(x_refs, first_tiles), g_ref[...]).astype(BF16)
    cos = cos_ref[...]
    sin = sin_ref[...]
    lane = lax.broadcasted_iota(jnp.int32, cos.shape, 1)
    first_half = (lane % HEAD_DIM) < (ROPE_DIM // 2)

    def proj(lo, hi):
        return jnp.dot(h, w_ref[:, lo:hi], preferred_element_type=F32)

    def rope(p):
        rot = jnp.where(first_half,
                        -pltpu.roll(p, LANES - ROPE_DIM // 2, 1),
                        pltpu.roll(p, ROPE_DIM // 2, 1))
        return p * cos + rot * sin

    o = 0
    p = proj(o, o + ATT_W)
    for s in range(N_SLABS):
        aq_ref[:, s * LANES:(s + 1) * LANES] = rope(p[:, s * LANES:(s + 1) * LANES]).astype(aq_ref.dtype)
    o += ATT_W
    p = proj(o, o + 2 * A_KV_W)
    ak_ref[...] = rope(p[:, :A_KV_W]).astype(ak_ref.dtype)
    av_ref[...] = p[:, A_KV_W:].astype(av_ref.dtype)
    o += 2 * A_KV_W
    for ref in (bq_ref, bk_ref, bv_ref):
        ref[...] = proj(o, o + ATT_W).astype(ref.dtype)
        o += ATT_W
    for s in range(4):
        c_ref[:, s * C_W:(s + 1) * C_W] = proj(o, o + C_W).astype(c_ref.dtype)
        o += C_W
    d_model = w_ref.shape[0]
    for s in range(3):
        gate_ref[:, s * d_model:(s + 1) * d_model] = proj(o, o + d_model).astype(gate_ref.dtype)
        o += d_model


def _inproj(xs, g, w, cos, sin, tm):
    rows = sum(x.shape[0] for x in xs)
    d_model, in_w = w.shape
    widths = (ATT_W, A_KV_W, A_KV_W, ATT_W, ATT_W, ATT_W, 4 * C_W, 3 * d_model)
    assert sum(widths) == in_w and rows % tm == 0
    row_block = lambda width: pl.BlockSpec((tm, width), lambda i: (i, 0))
    x_specs, first_tiles = _stacked_specs(xs, tm)
    return pl.pallas_call(
        functools.partial(_inproj_kernel, n_x=len(xs), first_tiles=first_tiles),
        grid=(rows // tm,),
        in_specs=x_specs + [_resident((1, d_model)), _resident((d_model, in_w)),
                            row_block(LANES), row_block(LANES)],
        out_specs=[row_block(wd) for wd in widths],
        out_shape=[jax.ShapeDtypeStruct((rows, wd), BF16) for wd in widths],
        compiler_params=_params("arbitrary"),
        name="inproj",
    )(*xs, g, w, cos, sin)


def _attn_kernel(*refs, nq, prev, k_slabs, use_bias, use_sink, mask_first, aliased):
    refs = list(refs)
    if aliased:
        refs.pop(0)
    q_ref, kp_ref, kc_ref, vp_ref, vc_ref, extra_ref, o_ref, kf_ref, vf_ref, s_ref, p_ref, rd_ref = refs
    hist = prev * CHUNK
    win = (prev + 1) * CHUNK
    pair = 2 * CHUNK
    kf_ref[0:hist, :] = kp_ref[...].astype(BF16)
    kf_ref[hist:hist + nq * CHUNK, :] = kc_ref[...]
    vf_ref[0:hist, :] = vp_ref[...].astype(BF16)
    vf_ref[hist:hist + nq * CHUNK, :] = vc_ref[...]

    lower_head = lax.broadcasted_iota(jnp.int32, (CHUNK, LANES), 1) < HEAD_DIM
    col = lax.broadcasted_iota(jnp.int32, (N_SLABS * pair, win), 1)
    is_first = pl.program_id(0) == 0
    k_lanes = lambda s: slice((s % k_slabs) * LANES, (s % k_slabs + 1) * LANES)

    for j in range(nq):
        for s in range(N_SLABS):
            q2 = q_ref[j * CHUNK:(j + 1) * CHUNK, s * LANES:(s + 1) * LANES]
            zero = jnp.zeros_like(q2)
            qq = jnp.concatenate([jnp.where(lower_head, q2, zero),
                                  jnp.where(lower_head, zero, q2)], axis=0)
            kw = kf_ref[j * CHUNK:j * CHUNK + win, k_lanes(s)]
            s_ref[j, s * pair:(s + 1) * pair, :] = lax.dot_general(
                qq, kw, (((1,), (1,)), ((), ())), preferred_element_type=F32)

    extra = extra_ref[...].reshape(N_SLABS * pair, extra_ref.shape[-1])
    for j in range(nq):
        sc = s_ref[j]
        if use_bias:
            sc = sc + extra
        if mask_first and j < prev:
            n_invalid = jnp.where(is_first, (prev - j) * CHUNK, 0)
            sc = jnp.where(col < n_invalid, NEG_INF, sc)
        m = jnp.max(sc, axis=-1, keepdims=True)
        if use_sink:
            m = jnp.maximum(m, extra)
        e = jnp.exp(sc - m)
        den = jnp.sum(e, axis=-1, keepdims=True)
        if use_sink:
            den = den + jnp.exp(extra - m)
        p_ref[j] = e.astype(BF16)
        rd_ref[j] = 1.0 / den

    for j in range(nq):
        for s in range(N_SLABS):
            vw = vf_ref[j * CHUNK:j * CHUNK + win, k_lanes(s)]
            o2 = jnp.dot(p_ref[j, s * pair:(s + 1) * pair, :], vw, preferred_element_type=F32)
            o2 = o2 * rd_ref[j, s * pair:(s + 1) * pair, :]
            o_ref[j * CHUNK:(j + 1) * CHUNK, s * LANES:(s + 1) * LANES] = jnp.where(
                lower_head, o2[:CHUNK], o2[CHUNK:]).astype(o_ref.dtype)


def _attention(q, k, v, extra, *, prev, use_bias, use_sink, nq, n_steps, q_block0,
               k_prev=None, v_prev=None, out=None, out_rows=None):
    kw = k.shape[1]
    hist = prev * CHUNK
    tq = nq * CHUNK
    from_cache = k_prev is not None
    cur = lambda width: pl.BlockSpec((tq, width), lambda i: (q_block0 + i, 0))
    if from_cache:
        prev_spec = pl.BlockSpec((hist, kw), lambda i: (i, 0))
        kp, vp = k_prev, v_prev
    else:
        assert tq % hist == 0 and q_block0 == 0
        ratio = tq // hist
        prev_spec = pl.BlockSpec((hist, kw), lambda i: (jnp.maximum(i * ratio - 1, 0), 0))
        kp, vp = k, v
    extra_spec = _resident(extra.shape)
    in_specs = [cur(ATT_W), prev_spec, cur(kw), prev_spec, cur(kw), extra_spec]
    args = [q, kp, k, vp, v, extra]
    aliases = {}
    if out is not None:
        in_specs = [pl.BlockSpec(memory_space=pl.ANY)] + in_specs
        args = [out] + args
        aliases = {0: 0}
        out_rows = out.shape[0]
    kern = functools.partial(_attn_kernel, nq=nq, prev=prev, k_slabs=kw // LANES, use_bias=use_bias,
                             use_sink=use_sink, mask_first=not from_cache, aliased=out is not None)
    return pl.pallas_call(
        kern,
        grid=(n_steps,),
        in_specs=in_specs,
        out_specs=cur(ATT_W),
        out_shape=jax.ShapeDtypeStruct((out_rows, ATT_W), BF16),
        scratch_shapes=[pltpu.VMEM((hist + tq, kw), BF16), pltpu.VMEM((hist + tq, kw), BF16),
                        pltpu.VMEM((nq, N_SLABS * 2 * CHUNK, hist + CHUNK), F32),
                        pltpu.VMEM((nq, N_SLABS * 2 * CHUNK, hist + CHUNK), BF16),
                        pltpu.VMEM((nq, N_SLABS * 2 * CHUNK, 1), F32)],
        input_output_aliases=aliases,
        compiler_params=_params("arbitrary"),
        name="attn_b" if use_bias else "attn_a",
    )(*args)


def _hgrn_kernel(*refs, layer, depth, aliased):
    refs = list(refs)
    if aliased:
        refs.pop(0)
    (c_ref, lbl_ref, cn_ref, s0_ref, o_ref, sfin_ref,
     st_ref, qd_ref, kd_ref, v_ref, dec_ref, oi_ref) = refs
    tm = c_ref.shape[0]
    nblk = tm // REC_BLOCK
    t = pl.program_id(1)

    @pl.when(t == 0)
    def _():
        for hd in range(C_HEADS):
            st_ref[hd] = s0_ref[0, hd].T

    logit_rows = [lbl_ref[i:i + 1, :] for i in range(depth)]
    mx = functools.reduce(jnp.maximum, logit_rows)
    ex = [jnp.exp(r - mx) for r in logit_rows]
    den = functools.reduce(jnp.add, ex)
    sm = [e / den for e in ex]
    lb_all = functools.reduce(jnp.add, sm[:layer + 1]) - sm[0]
    log_lb = jnp.log(lb_all)
    log_1mlb = jnp.log1p(-lb_all)

    pos = lax.broadcasted_iota(jnp.int32, (tm, C_DIM), 0) % REC_BLOCK

    o_intra = []
    og_all = []
    for hd in range(C_HEADS):
        lanes = slice(hd * C_DIM, (hd + 1) * C_DIM)
        z = c_ref[:, lanes].astype(F32)
        vv = c_ref[:, C_W + hd * C_DIM:C_W + (hd + 1) * C_DIM].astype(F32)
        qp = c_ref[:, 2 * C_W + hd * C_DIM:2 * C_W + (hd + 1) * C_DIM].astype(F32)
        q = qp * jax.nn.sigmoid(qp)
        log_sig = -(jnp.maximum(-z, 0.0) + jnp.log1p(jnp.exp(-jnp.abs(z))))
        a = log_lb[:, lanes]
        b = log_1mlb[:, lanes] + log_sig
        lf = jnp.maximum(a, b) + jnp.log1p(jnp.exp(-jnp.abs(a - b)))
        kk = 1.0 - jnp.exp(lf)
        cum = lf
        for sft in (1, 2, 4, 8):
            cum = cum + jnp.where(pos >= sft, pltpu.roll(cum, sft, 0), 0.0)
        last = jnp.where(pos == REC_BLOCK - 1, cum, 0.0)
        for sft in (1, 2, 4, 8):
            last = last + jnp.where(pos < REC_BLOCK - sft, pltpu.roll(last, tm - sft, 0), 0.0)
        qd_ref[hd] = (q * jnp.exp(cum)).astype(BF16)
        kd_ref[hd] = (kk * jnp.exp(last - cum)).astype(BF16)
        v_ref[hd] = vv.astype(BF16)
        dec_ref[hd] = jnp.exp(last)
        acc = jnp.zeros((tm, C_DIM), F32)
        for d in range(REC_BLOCK):
            kr = kk if d == 0 else pltpu.roll(kk, d, 0)
            cr = cum if d == 0 else pltpu.roll(cum, d, 0)
            vr = vv if d == 0 else pltpu.roll(vv, d, 0)
            w = q * kr * jnp.exp(jnp.minimum(cum - cr, 0.0))
            score = jnp.sum(w, axis=-1, keepdims=True)
            acc = acc + jnp.where(pos >= d, score, 0.0) * vr
        o_intra.append(acc)
        og_all.append(c_ref[:, 3 * C_W + hd * C_DIM:3 * C_W + (hd + 1) * C_DIM].astype(F32))

    def block_step(n, carry):
        r0 = pl.multiple_of(n * REC_BLOCK, REC_BLOCK)
        for hd in range(C_HEADS):
            st = st_ref[hd]
            qd = qd_ref[hd, pl.ds(r0, REC_BLOCK), :]
            kd = kd_ref[hd, pl.ds(r0, REC_BLOCK), :]
            vb = v_ref[hd, pl.ds(r0, REC_BLOCK), :]
            oi_ref[hd, pl.ds(r0, REC_BLOCK), :] = lax.dot_general(
                qd, st.astype(BF16), (((1,), (1,)), ((), ())), preferred_element_type=F32)
            upd = lax.dot_general(vb, kd, (((0,), (0,)), ((), ())), preferred_element_type=F32)
            dec = dec_ref[hd, pl.ds(r0, 8), :]
            st_ref[hd] = st * jnp.concatenate([dec] * (C_DIM // 8), axis=0) + upd
        return carry

    lax.fori_loop(0, nblk, block_step, 0)

    for hd in range(C_HEADS):
        lanes = slice(hd * C_DIM, (hd + 1) * C_DIM)
        o = o_intra[hd] + oi_ref[hd]
        o = o * lax.rsqrt(jnp.mean(o * o, axis=-1, keepdims=True) + NORM_EPS) * cn_ref[:, lanes]
        og = og_all[hd]
        o_ref[:, lanes] = (o * (og * jax.nn.sigmoid(og))).astype(o_ref.dtype)

    @pl.when(t == pl.num_programs(1) - 1)
    def _():
        for hd in range(C_HEADS):
            sfin_ref[0, hd] = st_ref[hd].T


def _hgrn(c, lb_logits, c_norm, s0, *, layer, tm, n_batch, n_tiles, row_block0, out=None, out_rows=None):
    depth = lb_logits.shape[0]
    rows_spec = lambda width: pl.BlockSpec((tm, width), lambda b, t: (row_block0 + b * n_tiles + t, 0))
    state_spec = pl.BlockSpec((1, C_HEADS, C_DIM, C_DIM), lambda b, t: (b, 0, 0, 0))
    in_specs = [rows_spec(4 * C_W), _resident(lb_logits.shape), _resident(c_norm.shape), state_spec]
    args = [c, lb_logits, c_norm, s0]
    aliases = {}
    if out is not None:
        in_specs = [pl.BlockSpec(memory_space=pl.ANY)] + in_specs
        args = [out] + args
        aliases = {0: 0}
        out_rows = out.shape[0]
    kern = functools.partial(_hgrn_kernel, layer=layer, depth=depth, aliased=out is not None)
    return pl.pallas_call(
        kern,
        grid=(n_batch, n_tiles),
        in_specs=in_specs,
        out_specs=[rows_spec(C_W), state_spec],
        out_shape=[jax.ShapeDtypeStruct((out_rows, C_W), BF16),
                   jax.ShapeDtypeStruct((n_batch, C_HEADS, C_DIM, C_DIM), F32)],
        scratch_shapes=[pltpu.VMEM((C_HEADS, C_DIM, C_DIM), F32),
                        pltpu.VMEM((C_HEADS, tm, C_DIM), BF16),
                        pltpu.VMEM((C_HEADS, tm, C_DIM), BF16),
                        pltpu.VMEM((C_HEADS, tm, C_DIM), BF16),
                        pltpu.VMEM((C_HEADS, tm, C_DIM), F32),
                        pltpu.VMEM((C_HEADS, tm, C_DIM), F32)],
        input_output_aliases=aliases,
        compiler_params=_params("arbitrary", "arbitrary"),
        name="hgrn2",
    )(*args)


def _merge_kernel(*refs, n_x, first_tiles):
    x_refs = refs[:n_x]
    oa_ref, ob_ref, oc_ref, ga_ref, gb_ref, gc_ref, wa_ref, wb_ref, wc_ref, wo_ref, y_ref = refs[n_x:]

    def branch(o_ref, w_ref, g_ref):
        return jax.nn.sigmoid(g_ref[...].astype(F32)) * jnp.dot(o_ref[...], w_ref[...], preferred_element_type=F32)

    mix = branch(oa_ref, wa_ref, ga_ref) + branch(ob_ref, wb_ref, gb_ref) + branch(oc_ref, wc_ref, gc_ref)
    y_ref[...] = _stacked_rows(x_refs, first_tiles) + jnp.dot(mix.astype(BF16), wo_ref[...],
                                                            preferred_element_type=F32)


def _merge(xs, oa, ob, oc, gates, wa, wb, wc, wo, tm):
    rows = sum(x.shape[0] for x in xs)
    d_model = wo.shape[0]
    row_block = lambda width, j=0: pl.BlockSpec((tm, width), lambda i: (i, j))
    x_specs, first_tiles = _stacked_specs(xs, tm)
    return pl.pallas_call(
        functools.partial(_merge_kernel, n_x=len(xs), first_tiles=first_tiles),
        grid=(rows // tm,),
        in_specs=x_specs + [row_block(ATT_W), row_block(ATT_W), row_block(C_W),
                            row_block(d_model, 0), row_block(d_model, 1), row_block(d_model, 2),
                            _resident(wa.shape), _resident(wb.shape), _resident(wc.shape), _resident(wo.shape)],
        out_specs=row_block(d_model),
        out_shape=jax.ShapeDtypeStruct((rows, d_model), F32),
        compiler_params=_params("arbitrary"),
        name="merge",
    )(*xs, oa, ob, oc, gates, gates, gates, wa, wb, wc, wo)


FFN_COLS = 256
EXT_PAD = 8


def _ffn_kernel(*refs, n_seg, seg_len, d_ff, use_state, final_norm, aliased):
    refs = list(refs)
    if aliased:
        refs.pop(0)
    x_ref, g2_ref, wup_ref, wconv_ref, bconv_ref, wdown_ref = refs[:6]
    refs = refs[6:]
    state_ref = refs.pop(0) if use_state else None
    gfin_ref = refs.pop(0) if final_norm else None
    y_ref, newconv_ref, ext_ref, carry_ref = refs
    cw = FFN_COLS
    tm = n_seg * seg_len
    i = pl.program_id(0)

    if not use_state:
        @pl.when(i == 0)
        def _():
            carry_ref[...] = jnp.zeros_like(carry_ref)

    x = x_ref[...]
    h = _rmsnorm(x, g2_ref[...]).astype(BF16)
    acc = jnp.zeros((tm, x.shape[1]), F32)
    for c in range(d_ff // cw):
        halves = []
        for col0 in (c * cw, d_ff + c * cw):
            cols = slice(col0, col0 + cw)
            up = jnp.dot(h, wup_ref[:, cols], preferred_element_type=F32)
            w0 = wconv_ref[0:1, cols]
            w1 = wconv_ref[1:2, cols]
            w2 = wconv_ref[2:3, cols]
            bias = bconv_ref[:, cols]
            segs = []
            for sg in range(n_seg):
                up_s = up[sg * seg_len:(sg + 1) * seg_len]
                if use_state:
                    before = state_ref[sg, :, cols]
                else:
                    before = carry_ref[EXT_PAD - 2:EXT_PAD, cols]
                ext_ref[sg, EXT_PAD - 2:EXT_PAD, :] = before
                ext_ref[sg, EXT_PAD:EXT_PAD + seg_len, :] = up_s
                conv = (bias + w0 * ext_ref[sg, EXT_PAD - 2:EXT_PAD - 2 + seg_len, :]
                        + w1 * ext_ref[sg, EXT_PAD - 1:EXT_PAD - 1 + seg_len, :] + w2 * up_s)
                segs.append(conv)
                tail = up_s[seg_len - 2:seg_len]
                if use_state:
                    newconv_ref[sg, :, cols] = tail
                else:
                    carry_ref[EXT_PAD - 2:EXT_PAD, cols] = tail
                    newconv_ref[0, :, cols] = tail
            halves.append(segs[0] if n_seg == 1 else jnp.concatenate(segs, axis=0))
        act = (halves[0] * jax.nn.gelu(halves[1])).astype(BF16)
        acc = acc + jnp.dot(act, wdown_ref[c * cw:(c + 1) * cw, :], preferred_element_type=F32)
    y = x + acc
    if final_norm:
        y = _rmsnorm(y, gfin_ref[...])
    y_ref[...] = y


def _ffn(x, g2, w_up, w_conv, b_conv, w_down, *, n_seg, seg_len, n_steps, row_block0,
         state=None, g_final=None, out=None, out_rows=None, out_block0=None):
    d_model = x.shape[1]
    d_ff = w_down.shape[0]
    assert d_ff % FFN_COLS == 0
    tm = n_seg * seg_len
    use_state = state is not None
    final_norm = g_final is not None
    out_block0 = row_block0 if out_block0 is None else out_block0
    in_specs = [pl.BlockSpec((tm, d_model), lambda i: (row_block0 + i, 0)),
                _resident(g2.shape), _resident(w_up.shape), _resident(w_conv.shape),
                _resident(b_conv.shape), _resident(w_down.shape)]
    args = [x, g2, w_up, w_conv, b_conv, w_down]
    if use_state:
        assert n_steps == 1
        in_specs.append(_resident(state.shape))
        args.append(state)
    if final_norm:
        in_specs.append(_resident(g_final.shape))
        args.append(g_final)
    aliases = {}
    if out is not None:
        in_specs = [pl.BlockSpec(memory_space=pl.ANY)] + in_specs
        args = [out] + args
        aliases = {0: 0}
        out_rows = out.shape[0]
    n_state = n_seg if use_state else 1
    kern = functools.partial(_ffn_kernel, n_seg=n_seg, seg_len=seg_len, d_ff=d_ff, use_state=use_state,
                             final_norm=final_norm, aliased=out is not None)
    return pl.pallas_call(
        kern,
        grid=(n_steps,),
        in_specs=in_specs,
        out_specs=[pl.BlockSpec((tm, d_model), lambda i: (out_block0 + i, 0)),
                   pl.BlockSpec((n_state, CONV_W - 1, 2 * d_ff), lambda i: (0, 0, 0))],
        out_shape=[jax.ShapeDtypeStruct((out_rows, d_model), F32),
                   jax.ShapeDtypeStruct((n_state, CONV_W - 1, 2 * d_ff), F32)],
        scratch_shapes=[pltpu.VMEM((n_seg, EXT_PAD + seg_len, FFN_COLS), F32),
                        pltpu.VMEM((EXT_PAD, 2 * d_ff), F32)],
        input_output_aliases=aliases,
        compiler_params=_params("arbitrary"),
        name="convffn",
    )(*args)


A_HEAD_ORDER = tuple(h for s in range(A_GROUP) for h in (s, A_GROUP + s))


def _rope_tables(pos):
    half = ROPE_DIM // 2
    inv_freq = ROPE_THETA ** (-jnp.arange(0, ROPE_DIM, 2, dtype=F32) / ROPE_DIM)
    ang = pos.astype(F32)[:, None] * inv_freq[None, :]
    d = jnp.arange(LANES) % HEAD_DIM
    rotated = d < ROPE_DIM
    cos = jnp.where(rotated[None, :], jnp.cos(ang)[:, d % half], 1.0)
    sin = jnp.where(rotated[None, :], jnp.sin(ang)[:, d % half], 0.0)
    return cos, sin


def _permute_a_heads(a, axis):
    parts = [lax.slice_in_dim(a, h * HEAD_DIM, (h + 1) * HEAD_DIM, axis=axis) for h in A_HEAD_ORDER]
    return jnp.concatenate(parts, axis=axis)


def _prep_w_in(w_in_l):
    scale = HEAD_DIM ** -0.5
    b0 = ATT_W + 2 * A_KV_W
    return jnp.concatenate([_permute_a_heads(w_in_l[:, :ATT_W], 1) * scale, w_in_l[:, ATT_W:b0],
                            w_in_l[:, b0:b0 + ATT_W] * scale, w_in_l[:, b0 + ATT_W:]], axis=1).astype(BF16)


def _rel_bias_pairs(table):
    assert CHUNK - 1 <= REL_CLIP
    reach = B_PREV_CHUNKS * CHUNK
    width = reach + CHUNK
    n_clipped = reach + CHUNK - 1 - REL_CLIP
    table = table.astype(F32)
    diag = jnp.concatenate([jnp.broadcast_to(table[:, 2 * REL_CLIP:], (B_HEADS, n_clipped)),
                            table[:, REL_CLIP - (CHUNK - 1):][:, ::-1]], axis=1)
    bias = jnp.stack([diag[:, CHUNK - 1 - q:CHUNK - 1 - q + width] for q in range(CHUNK)], axis=1)
    return bias.reshape(B_HEADS // 2, 2 * CHUNK, width)


def _sink_pairs(sinks):
    s = jnp.stack([sinks[h] for h in A_HEAD_ORDER]).astype(F32)
    return jnp.broadcast_to(s[:, None], (A_Q_HEADS, CHUNK)).reshape(N_SLABS, 2 * CHUNK, 1)


def kernel(x_prompt, x_sample, cache_a_k, cache_a_v, cache_b_k, cache_b_v, state_c, state_ffn_conv,
           norm1, w_in, a_sinks, b_rel_bias, c_lb_logits, c_norm, w_branch_a, w_branch_b, w_branch_c,
           w_out, norm2, w_up, w_conv, b_conv, w_down, norm_final):
    batch, seq, d_model = x_prompt.shape
    dec_batch, dec_seq, _ = x_sample.shape
    depth = w_in.shape[0]
    d_ff = w_down.shape[1]
    ra = cache_a_k.shape[2]
    rb = cache_b_k.shape[2]
    tm = 512
    assert batch == 1 and dec_seq == CHUNK and seq % tm == 0 and (dec_batch * dec_seq) % tm == 0
    assert ra == A_PREV_CHUNKS * CHUNK and rb == B_PREV_CHUNKS * CHUNK
    n_prompt = batch * seq
    n_sample = dec_batch * dec_seq
    rows = n_prompt + n_sample
    p_tiles = n_prompt // tm

    xs = [x_prompt.reshape(n_prompt, d_model), x_sample.reshape(n_sample, d_model)]
    pos = jnp.concatenate([jnp.arange(seq, dtype=jnp.int32),
                           jnp.tile(PAST_LEN + jnp.arange(dec_seq, dtype=jnp.int32), dec_batch)])
    cos, sin = _rope_tables(pos)
    lb_logits = c_lb_logits.astype(F32)
    zero_state = jnp.zeros((batch, C_HEADS, C_DIM, C_DIM), F32)

    prompt_states = []
    sample_states = []
    y_prompt = y_sample = None
    for l in range(depth):
        wa = _permute_a_heads(w_branch_a[l], 0)
        aq, ak, av, bq, bk, bv, c, gates = _inproj(
            xs, norm1[l].reshape(1, d_model), _prep_w_in(w_in[l]), cos, sin, tm=256)

        sinks = _sink_pairs(a_sinks[l])
        nq = tm // CHUNK
        oa = _attention(aq, ak, av, sinks, prev=A_PREV_CHUNKS, use_bias=False, use_sink=True,
                        nq=nq, n_steps=p_tiles, q_block0=0, out_rows=rows)
        oa = _attention(aq, ak, av, sinks, prev=A_PREV_CHUNKS, use_bias=False, use_sink=True,
                        nq=1, n_steps=dec_batch, q_block0=n_prompt // CHUNK,
                        k_prev=cache_a_k[l].reshape(dec_batch * ra, A_KV_W),
                        v_prev=cache_a_v[l].reshape(dec_batch * ra, A_KV_W), out=oa)
        bias = _rel_bias_pairs(b_rel_bias[l])
        ob = _attention(bq, bk, bv, bias, prev=B_PREV_CHUNKS, use_bias=True, use_sink=False,
                        nq=nq, n_steps=p_tiles, q_block0=0, out_rows=rows)
        ob = _attention(bq, bk, bv, bias, prev=B_PREV_CHUNKS, use_bias=True, use_sink=False,
                        nq=1, n_steps=dec_batch, q_block0=n_prompt // CHUNK,
                        k_prev=cache_b_k[l].reshape(dec_batch * rb, ATT_W),
                        v_prev=cache_b_v[l].reshape(dec_batch * rb, ATT_W), out=ob)
        cn = c_norm[l].reshape(1, C_W).astype(F32)
        oc, c_state_p = _hgrn(c, lb_logits, cn, zero_state, layer=l, tm=tm, n_batch=batch,
                              n_tiles=p_tiles, row_block0=0, out_rows=rows)
        oc, c_state_s = _hgrn(c, lb_logits, cn, state_c[l].astype(F32), layer=l, tm=dec_seq, n_batch=dec_batch,
                              n_tiles=1, row_block0=n_prompt // dec_seq, out=oc)

        x1 = _merge(xs, oa, ob, oc, gates, wa.astype(BF16), w_branch_b[l].astype(BF16),
                    w_branch_c[l].astype(BF16), w_out[l].astype(BF16), tm=tm)

        last = l == depth - 1
        g_final = norm_final.reshape(1, d_model) if last else None
        ffn_w = (norm2[l].reshape(1, d_model), w_up[l].astype(BF16), w_conv[l], b_conv[l].reshape(1, 2 * d_ff),
                 w_down[l].astype(BF16))
        if last:
            y_prompt, conv_p = _ffn(x1, *ffn_w, n_seg=1, seg_len=tm, n_steps=p_tiles, row_block0=0,
                                    g_final=g_final, out_rows=n_prompt)
            y_sample, conv_s = _ffn(x1, *ffn_w, n_seg=dec_batch, seg_len=dec_seq, n_steps=1, row_block0=p_tiles,
                                    state=state_ffn_conv[l], g_final=g_final, out_rows=n_sample, out_block0=0)
        else:
            x2, conv_p = _ffn(x1, *ffn_w, n_seg=1, seg_len=tm, n_steps=p_tiles, row_block0=0, out_rows=rows)
            x3, conv_s = _ffn(x1, *ffn_w, n_seg=dec_batch, seg_len=dec_seq, n_steps=1, row_block0=p_tiles,
                              state=state_ffn_conv[l], out=x2)
            xs = [x3]

        ak_s = ak[n_prompt:].astype(F32).reshape(dec_batch, dec_seq, A_KV_HEADS, HEAD_DIM)
        av_s = av[n_prompt:].astype(F32).reshape(dec_batch, dec_seq, A_KV_HEADS, HEAD_DIM)
        bk_s = bk[n_prompt:].astype(F32).reshape(dec_batch, dec_seq, B_HEADS, HEAD_DIM)
        bv_s = bv[n_prompt:].astype(F32).reshape(dec_batch, dec_seq, B_HEADS, HEAD_DIM)
        prompt_states.append((
            ak[n_prompt - ra:n_prompt].astype(F32).reshape(batch, ra, A_KV_HEADS, HEAD_DIM),
            av[n_prompt - ra:n_prompt].astype(F32).reshape(batch, ra, A_KV_HEADS, HEAD_DIM),
            bk[n_prompt - rb:n_prompt].astype(F32).reshape(batch, rb, B_HEADS, HEAD_DIM),
            bv[n_prompt - rb:n_prompt].astype(F32).reshape(batch, rb, B_HEADS, HEAD_DIM),
            c_state_p, conv_p))
        sample_states.append((
            jnp.concatenate([cache_a_k[l], ak_s], axis=1)[:, dec_seq:],
            jnp.concatenate([cache_a_v[l], av_s], axis=1)[:, dec_seq:],
            jnp.concatenate([cache_b_k[l], bk_s], axis=1)[:, dec_seq:],
            jnp.concatenate([cache_b_v[l], bv_s], axis=1)[:, dec_seq:],
            c_state_s, conv_s))

    def stk(sts, i):
        return jnp.stack([s[i] for s in sts], axis=0)

    return (y_prompt.reshape(batch, seq, d_model), y_sample.reshape(dec_batch, dec_seq, d_model),
            stk(prompt_states, 0), stk(prompt_states, 1), stk(prompt_states, 2), stk(prompt_states, 3),
            stk(prompt_states, 4), stk(prompt_states, 5),
            stk(sample_states, 0), stk(sample_states, 1), stk(sample_states, 2), stk(sample_states, 3),
            stk(sample_states, 4), stk(sample_states, 5))
```

```python
import functools

import jax
import jax.numpy as jnp
from jax import lax
from jax.experimental import pallas as pl
from jax.experimental.pallas import tpu as pltpu

F32 = jnp.float32
BF16 = jnp.bfloat16

CHUNK = 64
HEAD_DIM = 64
LANES = 128
A_Q_HEADS = 8
A_KV_HEADS = 2
A_GROUP = A_Q_HEADS // A_KV_HEADS
A_PREV_CHUNKS = 2
B_HEADS = 8
B_PREV_CHUNKS = 8
REL_CLIP = 128
ROPE_DIM = HEAD_DIM // 4
ROPE_THETA = 500000.0
C_HEADS = 4
C_DIM = 128
REC_BLOCK = 16
CONV_W = 3
NORM_EPS = 1e-6
NEG_INF = -1e30
PAST_LEN = 1024

ATT_W = A_Q_HEADS * HEAD_DIM
A_KV_W = A_KV_HEADS * HEAD_DIM
C_W = C_HEADS * C_DIM
N_SLABS = ATT_W // LANES
PAIR = 2 * CHUNK

TM = 512
VMEM_LIMIT_BYTES = 56 * 1024 * 1024


def _params(*sem):
    return pltpu.CompilerParams(dimension_semantics=sem, vmem_limit_bytes=VMEM_LIMIT_BYTES)


def _resident(shape):
    nd = len(shape)
    return pl.BlockSpec(tuple(shape), lambda *_: (0,) * nd, pipeline_mode=pl.Buffered(1))


def _layer_resident(arr, layer):
    nd = arr.ndim - 1
    return pl.BlockSpec((None,) + tuple(arr.shape[1:]), lambda *_: (layer,) + (0,) * nd,
                        pipeline_mode=pl.Buffered(1))


def _rmsnorm(x, g):
    return x * lax.rsqrt(jnp.mean(x * x, axis=-1, keepdims=True) + NORM_EPS) * g


def _stacked_rows(x_refs, first_tiles):
    if len(x_refs) == 1:
        return x_refs[0][...]
    return jnp.where(pl.program_id(0) < first_tiles, x_refs[0][...], x_refs[1][...])


def _stacked_specs(xs, tm):
    if len(xs) == 1:
        return [pl.BlockSpec((tm, xs[0].shape[1]), lambda i: (i, 0))], 0
    first_tiles = xs[0].shape[0] // tm
    assert xs[0].shape[0] % tm == 0 and xs[1].shape[0] % tm == 0
    return [pl.BlockSpec((tm, xs[0].shape[1]), lambda i: (jnp.minimum(i, first_tiles - 1), 0)),
            pl.BlockSpec((tm, xs[1].shape[1]), lambda i: (jnp.maximum(i - first_tiles, 0), 0))], first_tiles


def _inproj_kernel(*refs, n_x, first_tiles):
    x_refs = refs[:n_x]
    (g_ref, w_ref, cos_ref, sin_ref,
     aq_ref, ak_ref, av_ref, bq_ref, bk_ref, bv_ref, c_ref, gate_ref) = refs[n_x:]
    h = _rmsnorm(_stacked_rows(x_refs, first_tiles), g_ref[...]).astype(BF16)
    cos = cos_ref[...]
    sin = sin_ref[...]
    lane = lax.broadcasted_iota(jnp.int32, cos.shape, 1)
    first_half = (lane % HEAD_DIM) < (ROPE_DIM // 2)

    def proj(lo, hi):
        return jnp.dot(h, w_ref[:, lo:hi], preferred_element_type=F32)

    def rope(p):
        rot = jnp.where(first_half,
                        -pltpu.roll(p, LANES - ROPE_DIM // 2, 1),
                        pltpu.roll(p, ROPE_DIM // 2, 1))
        return p * cos + rot * sin

    o = 0
    p = proj(o, o + ATT_W)
    for s in range(N_SLABS):
        aq_ref[:, s * LANES:(s + 1) * LANES] = rope(p[:, s * LANES:(s + 1) * LANES]).astype(aq_ref.dtype)
    o += ATT_W
    p = proj(o, o + 2 * A_KV_W)
    ak_ref[...] = rope(p[:, :A_KV_W]).astype(ak_ref.dtype)
    av_ref[...] = p[:, A_KV_W:].astype(av_ref.dtype)
    o += 2 * A_KV_W
    for ref in (bq_ref, bk_ref, bv_ref):
        ref[...] = proj(o, o + ATT_W).astype(ref.dtype)
        o += ATT_W
    for s in range(4):
        c_ref[:, s * C_W:(s + 1) * C_W] = proj(o, o + C_W).astype(c_ref.dtype)
        o += C_W
    d_model = w_ref.shape[0]
    for s in range(3):
        gate_ref[:, s * d_model:(s + 1) * d_model] = proj(o, o + d_model).astype(gate_ref.dtype)
        o += d_model


def _inproj(xs, g, w, cos, sin, *, layer, tm):
    rows = sum(x.shape[0] for x in xs)
    _, d_model, in_w = w.shape
    widths = (ATT_W, A_KV_W, A_KV_W, ATT_W, ATT_W, ATT_W, 4 * C_W, 3 * d_model)
    assert sum(widths) == in_w and rows % tm == 0
    row_block = lambda width: pl.BlockSpec((tm, width), lambda i: (i, 0))
    x_specs, first_tiles = _stacked_specs(xs, tm)
    return pl.pallas_call(
        functools.partial(_inproj_kernel, n_x=len(xs), first_tiles=first_tiles),
        grid=(rows // tm,),
        in_specs=x_specs + [_layer_resident(g, layer), _layer_resident(w, layer),
                            row_block(LANES), row_block(LANES)],
        out_specs=[row_block(wd) for wd in widths],
        out_shape=[jax.ShapeDtypeStruct((rows, wd), BF16) for wd in widths],
        compiler_params=_params("arbitrary"),
        name="inproj",
    )(*xs, g, w, cos, sin)


def _attn_band(q_ref, kf_ref, vf_ref, extra_ref, o_ref, s_ref, p_ref, rd_ref, *,
               nq, win, starts, k_slabs, use_bias, use_sink, n_invalid):
    lower_head = lax.broadcasted_iota(jnp.int32, (CHUNK, LANES), 1) < HEAD_DIM
    col = lax.broadcasted_iota(jnp.int32, (N_SLABS * PAIR, win), 1)
    k_lanes = lambda s: slice((s % k_slabs) * LANES, (s % k_slabs + 1) * LANES)

    for j in range(nq):
        for s in range(N_SLABS):
            q2 = q_ref[j * CHUNK:(j + 1) * CHUNK, s * LANES:(s + 1) * LANES]
            zero = jnp.zeros_like(q2)
            qq = jnp.concatenate([jnp.where(lower_head, q2, zero),
                                  jnp.where(lower_head, zero, q2)], axis=0)
            kw = kf_ref[starts[j]:starts[j] + win, k_lanes(s)]
            s_ref[j, s * PAIR:(s + 1) * PAIR, :] = lax.dot_general(
                qq, kw, (((1,), (1,)), ((), ())), preferred_element_type=F32)

    extra = extra_ref[...].reshape(N_SLABS * PAIR, extra_ref.shape[-1])
    for j in range(nq):
        sc = s_ref[j]
        if use_bias:
            sc = sc + extra
        if n_invalid[j] is not None:
            sc = jnp.where(col < n_invalid[j], NEG_INF, sc)
        m = jnp.max(sc, axis=-1, keepdims=True)
        if use_sink:
            m = jnp.maximum(m, extra)
        e = jnp.exp(sc - m)
        den = jnp.sum(e, axis=-1, keepdims=True)
        if use_sink:
            den = den + jnp.exp(extra - m)
        p_ref[j] = e.astype(BF16)
        rd_ref[j] = 1.0 / den

    for j in range(nq):
        for s in range(N_SLABS):
            vw = vf_ref[starts[j]:starts[j] + win, k_lanes(s)]
            o2 = jnp.dot(p_ref[j, s * PAIR:(s + 1) * PAIR, :], vw, preferred_element_type=F32)
            o2 = o2 * rd_ref[j, s * PAIR:(s + 1) * PAIR, :]
            o_ref[j * CHUNK:(j + 1) * CHUNK, s * LANES:(s + 1) * LANES] = jnp.where(
                lower_head, o2[:CHUNK], o2[CHUNK:]).astype(o_ref.dtype)


def _attn_kernel(q_ref, kp_ref, kc_ref, vp_ref, vc_ref, ck_ref, cv_ref, extra_ref, o_ref,
                 kf_ref, vf_ref, s_ref, p_ref, rd_ref, *, prev, p_tiles, k_slabs, use_bias, use_sink):
    hist = prev * CHUNK
    win = hist + CHUNK
    nq = TM // CHUNK
    i = pl.program_id(0)
    band = functools.partial(_attn_band, q_ref, kf_ref, vf_ref, extra_ref, o_ref, s_ref, p_ref, rd_ref,
                             nq=nq, win=win, k_slabs=k_slabs, use_bias=use_bias, use_sink=use_sink)

    @pl.when(i < p_tiles)
    def _():
        kf_ref[0:hist, :] = kp_ref[...]
        kf_ref[hist:hist + TM, :] = kc_ref[...]
        vf_ref[0:hist, :] = vp_ref[...]
        vf_ref[hist:hist + TM, :] = vc_ref[...]
        n_invalid = [jnp.where(i == 0, (prev - j) * CHUNK, 0) if j < prev else None for j in range(nq)]
        band(starts=[j * CHUNK for j in range(nq)], n_invalid=n_invalid)

    @pl.when(i == p_tiles)
    def _():
        for b in range(nq):
            r0 = b * win
            kf_ref[r0:r0 + hist, :] = ck_ref[b * hist:(b + 1) * hist, :].astype(BF16)
            kf_ref[r0 + hist:r0 + win, :] = kc_ref[b * CHUNK:(b + 1) * CHUNK, :]
            vf_ref[r0:r0 + hist, :] = cv_ref[b * hist:(b + 1) * hist, :].astype(BF16)
            vf_ref[r0 + hist:r0 + win, :] = vc_ref[b * CHUNK:(b + 1) * CHUNK, :]
        band(starts=[b * win for b in range(nq)], n_invalid=[None] * nq)


def _attention(q, k, v, cache_k, cache_v, extra, *, layer, prev, use_bias, use_sink, p_tiles):
    rows = q.shape[0]
    kw = k.shape[1]
    hist = prev * CHUNK
    win = hist + CHUNK
    nq = TM // CHUNK
    assert TM % hist == 0 and rows == (p_tiles + 1) * TM and cache_k.shape[1:] == (nq * hist, kw)
    ratio = TM // hist
    cur = lambda width: pl.BlockSpec((TM, width), lambda i: (i, 0))
    prev_spec = pl.BlockSpec((hist, kw), lambda i: (jnp.maximum(i * ratio - 1, 0), 0))
    kern = functools.partial(_attn_kernel, prev=prev, p_tiles=p_tiles, k_slabs=kw // LANES,
                             use_bias=use_bias, use_sink=use_sink)
    return pl.pallas_call(
        kern,
        grid=(p_tiles + 1,),
        in_specs=[cur(ATT_W), prev_spec, cur(kw), prev_spec, cur(kw),
                  _layer_resident(cache_k, layer), _layer_resident(cache_v, layer), _resident(extra.shape)],
        out_specs=cur(ATT_W),
        out_shape=jax.ShapeDtypeStruct((rows, ATT_W), BF16),
        scratch_shapes=[pltpu.VMEM((nq * win, kw), BF16), pltpu.VMEM((nq * win, kw), BF16),
                        pltpu.VMEM((nq, N_SLABS * PAIR, win), F32),
                        pltpu.VMEM((nq, N_SLABS * PAIR, win), BF16),
                        pltpu.VMEM((nq, N_SLABS * PAIR, 1), F32)],
        compiler_params=_params("arbitrary"),
        name="attn_b" if use_bias else "attn_a",
    )(q, k, k, v, v, cache_k, cache_v, extra)


HGRN_SAFE_LOG_DECAY = -60.0
HGRN_GROUP = 128
HGRN_UNROLL = 8


def _hgrn_kernel(c_ref, lbl_ref, cn_ref, s0_ref, o_ref, sfp_ref, sfs_ref,
                 st_ref, qd_ref, kd_ref, v_ref, dec_ref, oi_ref, oin_ref, u_ref, sall_ref,
                 *, layer, depth, p_tiles, n_dec, dec_seq):
    tm = c_ref.shape[0]
    nblk = tm // REC_BLOCK
    i = pl.program_id(0)

    @pl.when(i == 0)
    def _():
        st_ref[...] = jnp.zeros_like(st_ref)

    logit_rows = [lbl_ref[r:r + 1, :] for r in range(depth)]
    mx = functools.reduce(jnp.maximum, logit_rows)
    ex = [jnp.exp(r - mx) for r in logit_rows]
    den = functools.reduce(jnp.add, ex)
    sm = [e / den for e in ex]
    lb_all = functools.reduce(jnp.add, sm[:layer + 1]) - sm[0]
    log_lb = jnp.log(lb_all)
    log_1mlb = jnp.log1p(-lb_all)

    pos = lax.broadcasted_iota(jnp.int32, (tm, C_DIM), 0) % REC_BLOCK
    grow = lax.broadcasted_iota(jnp.int32, (HGRN_GROUP, HGRN_GROUP), 0)
    gcol = lax.broadcasted_iota(jnp.int32, (HGRN_GROUP, HGRN_GROUP), 1)
    same_block_causal = jnp.logical_and(grow // REC_BLOCK == gcol // REC_BLOCK, gcol <= grow)

    for hd in range(C_HEADS):
        lanes = slice(hd * C_DIM, (hd + 1) * C_DIM)
        z = c_ref[:, lanes].astype(F32)
        vv = c_ref[:, C_W + hd * C_DIM:C_W + (hd + 1) * C_DIM].astype(F32)
        qp = c_ref[:, 2 * C_W + hd * C_DIM:2 * C_W + (hd + 1) * C_DIM].astype(F32)
        q = qp * jax.nn.sigmoid(qp)
        log_sig = -(jnp.maximum(-z, 0.0) + jnp.log(1.0 + jnp.exp(-jnp.abs(z))))
        a = log_lb[:, lanes]
        b = log_1mlb[:, lanes] + log_sig
        lf = jnp.maximum(a, b) + jnp.log(1.0 + jnp.exp(-jnp.abs(a - b)))
        kk = 1.0 - jnp.exp(lf)
        cum = lf
        for sft in (1, 2, 4, 8):
            cum = cum + jnp.where(pos >= sft, pltpu.roll(cum, sft, 0), 0.0)
        last = jnp.where(pos == REC_BLOCK - 1, cum, 0.0)
        for sft in (1, 2, 4, 8):
            last = last + jnp.where(pos < REC_BLOCK - sft, pltpu.roll(last, tm - sft, 0), 0.0)
        qd = (q * jnp.exp(cum)).astype(BF16)
        v16 = vv.astype(BF16)
        qd_ref[hd] = qd
        kd_ref[hd] = (kk * jnp.exp(last - cum)).astype(BF16)
        v_ref[hd] = v16
        dec_ref[hd] = jnp.exp(last)

        safe = jnp.min(cum) > HGRN_SAFE_LOG_DECAY

        @pl.when(safe)
        def _():
            kinv = (kk * jnp.exp(-cum)).astype(BF16)
            for g in range(tm // HGRN_GROUP):
                r = slice(g * HGRN_GROUP, (g + 1) * HGRN_GROUP)
                sc = lax.dot_general(qd[r], kinv[r], (((1,), (1,)), ((), ())), preferred_element_type=F32)
                sc = jnp.where(same_block_causal, sc, 0.0)
                oin_ref[hd, r, :] = jnp.dot(sc.astype(BF16), v16[r], preferred_element_type=F32)

        @pl.when(jnp.logical_not(safe))
        def _():
            acc = jnp.zeros((tm, C_DIM), F32)
            for d in range(REC_BLOCK):
                kr = kk if d == 0 else pltpu.roll(kk, d, 0)
                cr = cum if d == 0 else pltpu.roll(cum, d, 0)
                vr = vv if d == 0 else pltpu.roll(vv, d, 0)
                w = q * kr * jnp.exp(jnp.minimum(cum - cr, 0.0))
                score = jnp.sum(w, axis=-1, keepdims=True)
                acc = acc + jnp.where(pos >= d, score, 0.0) * vr
            oin_ref[hd] = acc

    def increment(n, carry):
        r0 = pl.multiple_of(n * REC_BLOCK, REC_BLOCK)
        for hd in range(C_HEADS):
            u_ref[hd, n] = lax.dot_general(v_ref[hd, pl.ds(r0, REC_BLOCK), :], kd_ref[hd, pl.ds(r0, REC_BLOCK), :],
                                           (((0,), (0,)), ((), ())), preferred_element_type=F32)
        return carry

    lax.fori_loop(0, nblk, increment, 0, unroll=HGRN_UNROLL)

    def advance(st, hd, n, r0):
        sall_ref[hd, n] = st.astype(BF16)
        dec = dec_ref[hd, pl.ds(r0, 8), :]
        return st * jnp.concatenate([dec] * (C_DIM // 8), axis=0) + u_ref[hd, n]

    @pl.when(i < p_tiles)
    def _():
        def step(n, carry):
            r0 = pl.multiple_of(n * REC_BLOCK, REC_BLOCK)
            for hd in range(C_HEADS):
                st_ref[hd] = advance(st_ref[hd], hd, n, r0)
            return carry

        lax.fori_loop(0, nblk, step, 0)

    @pl.when(i == p_tiles - 1)
    def _():
        for hd in range(C_HEADS):
            sfp_ref[0, hd] = st_ref[hd].T

    @pl.when(i == p_tiles)
    def _():
        per_stream = dec_seq // REC_BLOCK
        for b in range(n_dec):
            for hd in range(C_HEADS):
                st = s0_ref[b, hd].T
                for n in range(b * per_stream, (b + 1) * per_stream):
                    st = advance(st, hd, n, n * REC_BLOCK)
                sfs_ref[b, hd] = st.T

    def readout(n, carry):
        r0 = pl.multiple_of(n * REC_BLOCK, REC_BLOCK)
        for hd in range(C_HEADS):
            oi_ref[hd, pl.ds(r0, REC_BLOCK), :] = lax.dot_general(
                qd_ref[hd, pl.ds(r0, REC_BLOCK), :], sall_ref[hd, n],
                (((1,), (1,)), ((), ())), preferred_element_type=F32)
        return carry

    lax.fori_loop(0, nblk, readout, 0, unroll=HGRN_UNROLL)

    for hd in range(C_HEADS):
        lanes = slice(hd * C_DIM, (hd + 1) * C_DIM)
        o = oin_ref[hd] + oi_ref[hd]
        o = o * lax.rsqrt(jnp.mean(o * o, axis=-1, keepdims=True) + NORM_EPS) * cn_ref[:, lanes]
        og = c_ref[:, 3 * C_W + hd * C_DIM:3 * C_W + (hd + 1) * C_DIM].astype(F32)
        o_ref[:, lanes] = (o * (og * jax.nn.sigmoid(og))).astype(o_ref.dtype)


def _hgrn(c, lb_logits, c_norm, s0, *, layer, p_tiles, dec_seq):
    rows = c.shape[0]
    depth, n_dec = s0.shape[:2]
    assert rows == (p_tiles + 1) * TM and n_dec * dec_seq == TM and dec_seq % REC_BLOCK == 0
    nblk = TM // REC_BLOCK
    state = (C_HEADS, C_DIM, C_DIM)
    kern = functools.partial(_hgrn_kernel, layer=layer, depth=depth, p_tiles=p_tiles, n_dec=n_dec, dec_seq=dec_seq)
    return pl.pallas_call(
        kern,
        grid=(p_tiles + 1,),
        in_specs=[pl.BlockSpec((TM, 4 * C_W), lambda i: (i, 0)), _resident(lb_logits.shape),
                  _layer_resident(c_norm, layer), _layer_resident(s0, layer)],
        out_specs=[pl.BlockSpec((TM, C_W), lambda i: (i, 0)),
                   pl.BlockSpec((1,) + state, lambda i: (0, 0, 0, 0)),
                   pl.BlockSpec((n_dec,) + state, lambda i: (0, 0, 0, 0))],
        out_shape=[jax.ShapeDtypeStruct((rows, C_W), BF16),
                   jax.ShapeDtypeStruct((1,) + state, F32),
                   jax.ShapeDtypeStruct((n_dec,) + state, F32)],
        scratch_shapes=[pltpu.VMEM(state, F32),
                        pltpu.VMEM((C_HEADS, TM, C_DIM), BF16),
                        pltpu.VMEM((C_HEADS, TM, C_DIM), BF16),
                        pltpu.VMEM((C_HEADS, TM, C_DIM), BF16),
                        pltpu.VMEM((C_HEADS, TM, C_DIM), F32),
                        pltpu.VMEM((C_HEADS, TM, C_DIM), F32),
                        pltpu.VMEM((C_HEADS, TM, C_DIM), F32),
                        pltpu.VMEM((C_HEADS, nblk, C_DIM, C_DIM), F32),
                        pltpu.VMEM((C_HEADS, nblk, C_DIM, C_DIM), BF16)],
        compiler_params=_params("arbitrary"),
        name="hgrn2",
    )(c, lb_logits, c_norm, s0)


def _merge_kernel(*refs, n_x, first_tiles):
    x_refs = refs[:n_x]
    oa_ref, ob_ref, oc_ref, ga_ref, gb_ref, gc_ref, wa_ref, wb_ref, wc_ref, wo_ref, y_ref = refs[n_x:]

    def branch(o_ref, w_ref, g_ref):
        return jax.nn.sigmoid(g_ref[...].astype(F32)) * jnp.dot(o_ref[...], w_ref[...], preferred_element_type=F32)

    mix = branch(oa_ref, wa_ref, ga_ref) + branch(ob_ref, wb_ref, gb_ref) + branch(oc_ref, wc_ref, gc_ref)
    y_ref[...] = _stacked_rows(x_refs, first_tiles) + jnp.dot(mix.astype(BF16), wo_ref[...],
                                                            preferred_element_type=F32)


def _merge(xs, oa, ob, oc, gates, wa, wb, wc, wo, *, layer, tm):
    rows = sum(x.shape[0] for x in xs)
    d_model = wo.shape[-1]
    row_block = lambda width, j=0: pl.BlockSpec((tm, width), lambda i: (i, j))
    x_specs, first_tiles = _stacked_specs(xs, tm)
    return pl.pallas_call(
        functools.partial(_merge_kernel, n_x=len(xs), first_tiles=first_tiles),
        grid=(rows // tm,),
        in_specs=x_specs + [row_block(ATT_W), row_block(ATT_W), row_block(C_W),
                            row_block(d_model, 0), row_block(d_model, 1), row_block(d_model, 2)]
        + [_layer_resident(w, layer) for w in (wa, wb, wc, wo)],
        out_specs=row_block(d_model),
        out_shape=jax.ShapeDtypeStruct((rows, d_model), F32),
        compiler_params=_params("arbitrary"),
        name="merge",
    )(*xs, oa, ob, oc, gates, gates, gates, wa, wb, wc, wo)


FFN_COLS = 256
EXT_PAD = 8


def _ffn_kernel(*refs, p_tiles, n_seg, seg_len, d_ff, final_norm, split_out):
    refs = list(refs)
    x_ref, g2_ref, wup_ref, wconv_ref, bconv_ref, wdown_ref, state_ref = refs[:7]
    refs = refs[7:]
    gfin_ref = refs.pop(0) if final_norm else None
    y_refs = [refs.pop(0) for _ in range(2 if split_out else 1)]
    newp_ref, news_ref, ext_ref, carry_ref = refs
    cw = FFN_COLS
    n_chunks = d_ff // cw
    tm = n_seg * seg_len
    i = pl.program_id(0)
    is_sample = i == p_tiles

    @pl.when(i == 0)
    def _():
        carry_ref[...] = jnp.zeros_like(carry_ref)

    x = x_ref[...]
    h = _rmsnorm(x, g2_ref[...]).astype(BF16)

    def up_proj(c):
        return [jnp.dot(h, wup_ref[:, col0:col0 + cw], preferred_element_type=F32)
                for col0 in (c * cw, d_ff + c * cw)]

    def conv(up, col0, slot):
        cols = slice(col0, col0 + cw)
        w0 = wconv_ref[0:1, cols]
        w1 = wconv_ref[1:2, cols]
        w2 = wconv_ref[2:3, cols]
        bias = bconv_ref[:, cols]
        segs = []
        for sg in range(n_seg):
            up_s = up[sg * seg_len:(sg + 1) * seg_len]
            before = carry_ref[EXT_PAD - 2:EXT_PAD, cols] if sg == 0 else up[sg * seg_len - 2:sg * seg_len]
            before = jnp.where(is_sample, state_ref[sg, :, cols], before)
            ext_ref[slot, sg, EXT_PAD - 2:EXT_PAD, :] = before
            ext_ref[slot, sg, EXT_PAD:EXT_PAD + seg_len, :] = up_s
            segs.append(bias + w0 * ext_ref[slot, sg, EXT_PAD - 2:EXT_PAD - 2 + seg_len, :]
                        + w1 * ext_ref[slot, sg, EXT_PAD - 1:EXT_PAD - 1 + seg_len, :] + w2 * up_s)
            news_ref[sg, :, cols] = up_s[seg_len - 2:seg_len]
        carry_ref[EXT_PAD - 2:EXT_PAD, cols] = up[tm - 2:tm]
        return jnp.concatenate(segs, axis=0)

    acc = jnp.zeros((tm, x.shape[1]), F32)
    ups = up_proj(0)
    for c in range(n_chunks):
        nxt = up_proj(c + 1) if c + 1 < n_chunks else None
        u = conv(ups[0], c * cw, 0)
        g = conv(ups[1], d_ff + c * cw, 1)
        act = (u * jax.nn.gelu(g)).astype(BF16)
        acc = acc + jnp.dot(act, wdown_ref[c * cw:(c + 1) * cw, :], preferred_element_type=F32)
        ups = nxt
    y = x + acc
    if final_norm:
        y = _rmsnorm(y, gfin_ref[...])

    if split_out:
        @pl.when(i < p_tiles)
        def _():
            y_refs[0][...] = y

        @pl.when(is_sample)
        def _():
            y_refs[1][...] = y
    else:
        y_refs[0][...] = y

    @pl.when(i == p_tiles - 1)
    def _():
        newp_ref[0] = carry_ref[EXT_PAD - 2:EXT_PAD, :]


def _ffn(x, g2, w_up, w_conv, b_conv, w_down, state, g_final, *, layer, p_tiles, dec_seq, split_out):
    rows, d_model = x.shape
    d_ff = w_down.shape[1]
    n_dec = state.shape[1]
    assert d_ff % FFN_COLS == 0 and n_dec * dec_seq == TM and rows == (p_tiles + 1) * TM
    final_norm = g_final is not None
    args = [x, g2, w_up, w_conv, b_conv, w_down, state]
    in_specs = [pl.BlockSpec((TM, d_model), lambda i: (i, 0))] + [_layer_resident(a, layer) for a in args[1:]]
    if final_norm:
        in_specs.append(_resident(g_final.shape))
        args.append(g_final)
    if split_out:
        y_specs = [pl.BlockSpec((TM, d_model), lambda i: (jnp.minimum(i, p_tiles - 1), 0)),
                   pl.BlockSpec((TM, d_model), lambda i: (0, 0))]
        y_shapes = [jax.ShapeDtypeStruct((p_tiles * TM, d_model), F32), jax.ShapeDtypeStruct((TM, d_model), F32)]
    else:
        y_specs = [pl.BlockSpec((TM, d_model), lambda i: (i, 0))]
        y_shapes = [jax.ShapeDtypeStruct((rows, d_model), F32)]
    conv_state = lambda n: (n, CONV_W - 1, 2 * d_ff)
    kern = functools.partial(_ffn_kernel, p_tiles=p_tiles, n_seg=n_dec, seg_len=dec_seq, d_ff=d_ff,
                             final_norm=final_norm, split_out=split_out)
    return pl.pallas_call(
        kern,
        grid=(p_tiles + 1,),
        in_specs=in_specs,
        out_specs=y_specs + [pl.BlockSpec(conv_state(1), lambda i: (0, 0, 0)),
                             pl.BlockSpec(conv_state(n_dec), lambda i: (0, 0, 0))],
        out_shape=y_shapes + [jax.ShapeDtypeStruct(conv_state(1), F32),
                              jax.ShapeDtypeStruct(conv_state(n_dec), F32)],
        scratch_shapes=[pltpu.VMEM((2, n_dec, EXT_PAD + dec_seq, FFN_COLS), F32),
                        pltpu.VMEM((EXT_PAD, 2 * d_ff), F32)],
        compiler_params=_params("arbitrary"),
        name="convffn",
    )(*args)


A_HEAD_ORDER = tuple(h for s in range(A_GROUP) for h in (s, A_GROUP + s))


def _rope_tables(pos):
    half = ROPE_DIM // 2
    inv_freq = ROPE_THETA ** (-jnp.arange(0, ROPE_DIM, 2, dtype=F32) / ROPE_DIM)
    ang = pos.astype(F32)[:, None] * inv_freq[None, :]
    rotated = (jnp.arange(LANES) % HEAD_DIM < ROPE_DIM)[None, :]
    tile = lambda t: jnp.tile(t, (1, LANES // half))
    return jnp.where(rotated, tile(jnp.cos(ang)), 1.0), jnp.where(rotated, tile(jnp.sin(ang)), 0.0)


def _permute_a_heads(a, axis):
    parts = [lax.slice_in_dim(a, h * HEAD_DIM, (h + 1) * HEAD_DIM, axis=axis) for h in A_HEAD_ORDER]
    return jnp.concatenate(parts, axis=axis)


def _prep_w_in(w_in):
    scale = HEAD_DIM ** -0.5
    b0 = ATT_W + 2 * A_KV_W
    return jnp.concatenate([_permute_a_heads(w_in[..., :ATT_W], w_in.ndim - 1) * scale, w_in[..., ATT_W:b0],
                            w_in[..., b0:b0 + ATT_W] * scale, w_in[..., b0 + ATT_W:]], axis=-1).astype(BF16)


def _rel_bias_pairs(table):
    assert CHUNK - 1 <= REL_CLIP
    reach = B_PREV_CHUNKS * CHUNK
    width = reach + CHUNK
    n_clipped = reach + CHUNK - 1 - REL_CLIP
    table = table.astype(F32)
    diag = jnp.concatenate([jnp.broadcast_to(table[:, 2 * REL_CLIP:], (B_HEADS, n_clipped)),
                            table[:, REL_CLIP - (CHUNK - 1):][:, ::-1]], axis=1)
    bias = jnp.stack([diag[:, CHUNK - 1 - q:CHUNK - 1 - q + width] for q in range(CHUNK)], axis=1)
    return bias.reshape(B_HEADS // 2, PAIR, width)


def _sink_pairs(sinks):
    s = jnp.stack([sinks[h] for h in A_HEAD_ORDER]).astype(F32)
    return jnp.broadcast_to(s[:, None], (A_Q_HEADS, CHUNK)).reshape(N_SLABS, PAIR, 1)


def kernel(x_prompt, x_sample, cache_a_k, cache_a_v, cache_b_k, cache_b_v, state_c, state_ffn_conv,
           norm1, w_in, a_sinks, b_rel_bias, c_lb_logits, c_norm, w_branch_a, w_branch_b, w_branch_c,
           w_out, norm2, w_up, w_conv, b_conv, w_down, norm_final):
    batch, seq, d_model = x_prompt.shape
    dec_batch, dec_seq, _ = x_sample.shape
    depth = w_in.shape[0]
    d_ff = w_down.shape[1]
    ra = cache_a_k.shape[2]
    rb = cache_b_k.shape[2]
    n_prompt = batch * seq
    n_sample = dec_batch * dec_seq
    assert batch == 1 and dec_seq == CHUNK and seq % TM == 0 and n_sample == TM
    assert ra == A_PREV_CHUNKS * CHUNK and rb == B_PREV_CHUNKS * CHUNK
    p_tiles = n_prompt // TM

    xs = [x_prompt.reshape(n_prompt, d_model), x_sample.reshape(n_sample, d_model)]
    pos = jnp.concatenate([jnp.arange(seq, dtype=jnp.int32),
                           jnp.tile(PAST_LEN + jnp.arange(dec_seq, dtype=jnp.int32), dec_batch)])
    cos, sin = _rope_tables(pos)

    w_in_k = _prep_w_in(w_in)
    wa_k = _permute_a_heads(w_branch_a, 1).astype(BF16)
    wb_k, wc_k, wo_k = w_branch_b.astype(BF16), w_branch_c.astype(BF16), w_out.astype(BF16)
    w_up_k, w_down_k = w_up.astype(BF16), w_down.astype(BF16)
    norm1_k = norm1.reshape(depth, 1, d_model)
    norm2_k = norm2.reshape(depth, 1, d_model)
    b_conv_k = b_conv.reshape(depth, 1, 2 * d_ff)
    c_norm_k = c_norm.reshape(depth, 1, C_W).astype(F32)
    lb_logits = c_lb_logits.astype(F32)
    ca_k = cache_a_k.reshape(depth, dec_batch * ra, A_KV_W)
    ca_v = cache_a_v.reshape(depth, dec_batch * ra, A_KV_W)
    cb_k = cache_b_k.reshape(depth, dec_batch * rb, ATT_W)
    cb_v = cache_b_v.reshape(depth, dec_batch * rb, ATT_W)
    state_c_k = state_c.astype(F32)

    new_kv = []
    c_states_p, c_states_s, convs_p, convs_s = [], [], [], []
    y_prompt = y_sample = None
    for l in range(depth):
        aq, ak, av, bq, bk, bv, c, gates = _inproj(xs, norm1_k, w_in_k, cos, sin, layer=l, tm=256)
        oa = _attention(aq, ak, av, ca_k, ca_v, _sink_pairs(a_sinks[l]), layer=l, prev=A_PREV_CHUNKS,
                        use_bias=False, use_sink=True, p_tiles=p_tiles)
        ob = _attention(bq, bk, bv, cb_k, cb_v, _rel_bias_pairs(b_rel_bias[l]), layer=l, prev=B_PREV_CHUNKS,
                        use_bias=True, use_sink=False, p_tiles=p_tiles)
        oc, c_state_p, c_state_s = _hgrn(c, lb_logits, c_norm_k, state_c_k, layer=l, p_tiles=p_tiles,
                                         dec_seq=dec_seq)
        x1 = _merge(xs, oa, ob, oc, gates, wa_k, wb_k, wc_k, wo_k, layer=l, tm=TM)
        last = l == depth - 1
        outs = _ffn(x1, norm2_k, w_up_k, w_conv, b_conv_k, w_down_k, state_ffn_conv,
                    norm_final.reshape(1, d_model) if last else None,
                    layer=l, p_tiles=p_tiles, dec_seq=dec_seq, split_out=last)
        if last:
            y_prompt, y_sample, conv_p, conv_s = outs
        else:
            x2, conv_p, conv_s = outs
            xs = [x2]
        new_kv.append((ak, av, bk, bv))
        c_states_p.append(c_state_p)
        c_states_s.append(c_state_s)
        convs_p.append(conv_p)
        convs_s.append(conv_s)

    def new_cache(idx, cache, reach, n_heads):
        prompt = jnp.stack([t[idx][n_prompt - reach:n_prompt] for t in new_kv]).astype(F32)
        fresh = jnp.stack([t[idx][n_prompt:] for t in new_kv]).astype(F32)
        return (prompt.reshape(depth, batch, reach, n_heads, HEAD_DIM),
                jnp.concatenate([cache[:, :, dec_seq:],
                                 fresh.reshape(depth, dec_batch, dec_seq, n_heads, HEAD_DIM)], axis=2))

    a_k_p, a_k_s = new_cache(0, cache_a_k, ra, A_KV_HEADS)
    a_v_p, a_v_s = new_cache(1, cache_a_v, ra, A_KV_HEADS)
    b_k_p, b_k_s = new_cache(2, cache_b_k, rb, B_HEADS)
    b_v_p, b_v_s = new_cache(3, cache_b_v, rb, B_HEADS)
    return (y_prompt.reshape(batch, seq, d_model), y_sample.reshape(dec_batch, dec_seq, d_model),
            a_k_p, a_v_p, b_k_p, b_v_p, jnp.stack(c_states_p), jnp.stack(convs_p),
            a_k_s, a_v_s, b_k_s, b_v_s, jnp.stack(c_states_s), jnp.stack(convs_s))
```

```python
import functools

import jax
import jax.numpy as jnp
from jax import lax
from jax.experimental import pallas as pl
from jax.experimental.pallas import tpu as pltpu

F32 = jnp.float32
BF16 = jnp.bfloat16

CHUNK = 64
HEAD_DIM = 64
LANES = 128
A_Q_HEADS = 8
A_KV_HEADS = 2
A_GROUP = A_Q_HEADS // A_KV_HEADS
A_PREV_CHUNKS = 2
B_HEADS = 8
B_PREV_CHUNKS = 8
REL_CLIP = 128
ROPE_DIM = HEAD_DIM // 4
ROPE_THETA = 500000.0
C_HEADS = 4
C_DIM = 128
REC_BLOCK = 16
CONV_W = 3
NORM_EPS = 1e-6
NEG_INF = -1e30
PAST_LEN = 1024

ATT_W = A_Q_HEADS * HEAD_DIM
A_KV_W = A_KV_HEADS * HEAD_DIM
C_W = C_HEADS * C_DIM
N_SLABS = ATT_W // LANES
PAIR = 2 * CHUNK

TM = 512
VMEM_LIMIT_BYTES = 56 * 1024 * 1024


def _params(*sem):
    return pltpu.CompilerParams(dimension_semantics=sem, vmem_limit_bytes=VMEM_LIMIT_BYTES)


def _resident(shape):
    nd = len(shape)
    return pl.BlockSpec(tuple(shape), lambda *_: (0,) * nd, pipeline_mode=pl.Buffered(1))


def _layer_resident(arr, layer):
    nd = arr.ndim - 1
    return pl.BlockSpec((None,) + tuple(arr.shape[1:]), lambda *_: (layer,) + (0,) * nd,
                        pipeline_mode=pl.Buffered(1))


def _rmsnorm(x, g):
    return x * lax.rsqrt(jnp.mean(x * x, axis=-1, keepdims=True) + NORM_EPS) * g


def _stacked_rows(x_refs, first_tiles):
    if len(x_refs) == 1:
        return x_refs[0][...]
    return jnp.where(pl.program_id(0) < first_tiles, x_refs[0][...], x_refs[1][...])


def _stacked_specs(xs, tm):
    if len(xs) == 1:
        return [pl.BlockSpec((tm, xs[0].shape[1]), lambda i: (i, 0))], 0
    first_tiles = xs[0].shape[0] // tm
    assert xs[0].shape[0] % tm == 0 and xs[1].shape[0] % tm == 0
    return [pl.BlockSpec((tm, xs[0].shape[1]), lambda i: (jnp.minimum(i, first_tiles - 1), 0)),
            pl.BlockSpec((tm, xs[1].shape[1]), lambda i: (jnp.maximum(i - first_tiles, 0), 0))], first_tiles


def _inproj_kernel(*refs, n_x, first_tiles):
    x_refs = refs[:n_x]
    (g_ref, w_ref, cos_ref, sin_ref,
     aq_ref, ak_ref, av_ref, bq_ref, bk_ref, bv_ref, c_ref, gate_ref) = refs[n_x:]
    h = _rmsnorm(_stacked_rows(x_refs, first_tiles), g_ref[...]).astype(BF16)
    cos = cos_ref[...]
    sin = sin_ref[...]
    lane = lax.broadcasted_iota(jnp.int32, cos.shape, 1)
    first_half = (lane % HEAD_DIM) < (ROPE_DIM // 2)

    def proj(lo, hi):
        return jnp.dot(h, w_ref[:, lo:hi], preferred_element_type=F32)

    def rope(p):
        rot = jnp.where(first_half,
                        -pltpu.roll(p, LANES - ROPE_DIM // 2, 1),
                        pltpu.roll(p, ROPE_DIM // 2, 1))
        return p * cos + rot * sin

    o = 0
    p = proj(o, o + ATT_W)
    for s in range(N_SLABS):
        aq_ref[:, s * LANES:(s + 1) * LANES] = rope(p[:, s * LANES:(s + 1) * LANES]).astype(aq_ref.dtype)
    o += ATT_W
    p = proj(o, o + 2 * A_KV_W)
    ak_ref[...] = rope(p[:, :A_KV_W]).astype(ak_ref.dtype)
    av_ref[...] = p[:, A_KV_W:].astype(av_ref.dtype)
    o += 2 * A_KV_W
    for ref in (bq_ref, bk_ref, bv_ref):
        ref[...] = proj(o, o + ATT_W).astype(ref.dtype)
        o += ATT_W
    for s in range(4):
        c_ref[:, s * C_W:(s + 1) * C_W] = proj(o, o + C_W).astype(c_ref.dtype)
        o += C_W
    d_model = w_ref.shape[0]
    for s in range(3):
        gate_ref[:, s * d_model:(s + 1) * d_model] = proj(o, o + d_model).astype(gate_ref.dtype)
        o += d_model


def _inproj(xs, g, w, cos, sin, *, layer, tm):
    rows = sum(x.shape[0] for x in xs)
    _, d_model, in_w = w.shape
    widths = (ATT_W, A_KV_W, A_KV_W, ATT_W, ATT_W, ATT_W, 4 * C_W, 3 * d_model)
    assert sum(widths) == in_w and rows % tm == 0
    row_block = lambda width: pl.BlockSpec((tm, width), lambda i: (i, 0))
    x_specs, first_tiles = _stacked_specs(xs, tm)
    return pl.pallas_call(
        functools.partial(_inproj_kernel, n_x=len(xs), first_tiles=first_tiles),
        grid=(rows // tm,),
        in_specs=x_specs + [_layer_resident(g, layer), _layer_resident(w, layer),
                            row_block(LANES), row_block(LANES)],
        out_specs=[row_block(wd) for wd in widths],
        out_shape=[jax.ShapeDtypeStruct((rows, wd), BF16) for wd in widths],
        compiler_params=_params("arbitrary"),
        name="inproj",
    )(*xs, g, w, cos, sin)


def _attn_band(q_ref, kf_ref, vf_ref, extra_ref, o_ref, s_ref, p_ref, rd_ref, *,
               nq, win, starts, k_slabs, use_bias, use_sink, n_invalid):
    lower_head = lax.broadcasted_iota(jnp.int32, (CHUNK, LANES), 1) < HEAD_DIM
    col = lax.broadcasted_iota(jnp.int32, (N_SLABS * PAIR, win), 1)
    k_lanes = lambda s: slice((s % k_slabs) * LANES, (s % k_slabs + 1) * LANES)

    for j in range(nq):
        for s in range(N_SLABS):
            q2 = q_ref[j * CHUNK:(j + 1) * CHUNK, s * LANES:(s + 1) * LANES]
            zero = jnp.zeros_like(q2)
            qq = jnp.concatenate([jnp.where(lower_head, q2, zero),
                                  jnp.where(lower_head, zero, q2)], axis=0)
            kw = kf_ref[starts[j]:starts[j] + win, k_lanes(s)]
            s_ref[j, s * PAIR:(s + 1) * PAIR, :] = lax.dot_general(
                qq, kw, (((1,), (1,)), ((), ())), preferred_element_type=F32)

    extra = extra_ref[...].reshape(N_SLABS * PAIR, extra_ref.shape[-1])
    for j in range(nq):
        sc = s_ref[j]
        if use_bias:
            sc = sc + extra
        if n_invalid[j] is not None:
            sc = jnp.where(col < n_invalid[j], NEG_INF, sc)
        m = jnp.max(sc, axis=-1, keepdims=True)
        if use_sink:
            m = jnp.maximum(m, extra)
        e = jnp.exp(sc - m)
        den = jnp.sum(e, axis=-1, keepdims=True)
        if use_sink:
            den = den + jnp.exp(extra - m)
        p_ref[j] = e.astype(BF16)
        rd_ref[j] = 1.0 / den

    for j in range(nq):
        for s in range(N_SLABS):
            vw = vf_ref[starts[j]:starts[j] + win, k_lanes(s)]
            o2 = jnp.dot(p_ref[j, s * PAIR:(s + 1) * PAIR, :], vw, preferred_element_type=F32)
            o2 = o2 * rd_ref[j, s * PAIR:(s + 1) * PAIR, :]
            o_ref[j * CHUNK:(j + 1) * CHUNK, s * LANES:(s + 1) * LANES] = jnp.where(
                lower_head, o2[:CHUNK], o2[CHUNK:]).astype(o_ref.dtype)


def _attn_kernel(q_ref, kp_ref, kc_ref, vp_ref, vc_ref, ck_ref, cv_ref, extra_ref, o_ref,
                 kf_ref, vf_ref, s_ref, p_ref, rd_ref, *, prev, p_tiles, k_slabs, use_bias, use_sink):
    hist = prev * CHUNK
    win = hist + CHUNK
    nq = TM // CHUNK
    i = pl.program_id(0)
    band = functools.partial(_attn_band, q_ref, kf_ref, vf_ref, extra_ref, o_ref, s_ref, p_ref, rd_ref,
                             nq=nq, win=win, k_slabs=k_slabs, use_bias=use_bias, use_sink=use_sink)

    @pl.when(i < p_tiles)
    def _():
        kf_ref[0:hist, :] = kp_ref[...]
        kf_ref[hist:hist + TM, :] = kc_ref[...]
        vf_ref[0:hist, :] = vp_ref[...]
        vf_ref[hist:hist + TM, :] = vc_ref[...]
        n_invalid = [jnp.where(i == 0, (prev - j) * CHUNK, 0) if j < prev else None for j in range(nq)]
        band(starts=[j * CHUNK for j in range(nq)], n_invalid=n_invalid)

    @pl.when(i == p_tiles)
    def _():
        for b in range(nq):
            r0 = b * win
            kf_ref[r0:r0 + hist, :] = ck_ref[b * hist:(b + 1) * hist, :].astype(BF16)
            kf_ref[r0 + hist:r0 + win, :] = kc_ref[b * CHUNK:(b + 1) * CHUNK, :]
            vf_ref[r0:r0 + hist, :] = cv_ref[b * hist:(b + 1) * hist, :].astype(BF16)
            vf_ref[r0 + hist:r0 + win, :] = vc_ref[b * CHUNK:(b + 1) * CHUNK, :]
        band(starts=[b * win for b in range(nq)], n_invalid=[None] * nq)


def _attention(q, k, v, cache_k, cache_v, extra, *, layer, prev, use_bias, use_sink, p_tiles):
    rows = q.shape[0]
    kw = k.shape[1]
    hist = prev * CHUNK
    win = hist + CHUNK
    nq = TM // CHUNK
    assert TM % hist == 0 and rows == (p_tiles + 1) * TM and cache_k.shape[1:] == (nq * hist, kw)
    ratio = TM // hist
    cur = lambda width: pl.BlockSpec((TM, width), lambda i: (i, 0))
    prev_spec = pl.BlockSpec((hist, kw), lambda i: (jnp.maximum(i * ratio - 1, 0), 0))
    kern = functools.partial(_attn_kernel, prev=prev, p_tiles=p_tiles, k_slabs=kw // LANES,
                             use_bias=use_bias, use_sink=use_sink)
    return pl.pallas_call(
        kern,
        grid=(p_tiles + 1,),
        in_specs=[cur(ATT_W), prev_spec, cur(kw), prev_spec, cur(kw),
                  _layer_resident(cache_k, layer), _layer_resident(cache_v, layer), _resident(extra.shape)],
        out_specs=cur(ATT_W),
        out_shape=jax.ShapeDtypeStruct((rows, ATT_W), BF16),
        scratch_shapes=[pltpu.VMEM((nq * win, kw), BF16), pltpu.VMEM((nq * win, kw), BF16),
                        pltpu.VMEM((nq, N_SLABS * PAIR, win), F32),
                        pltpu.VMEM((nq, N_SLABS * PAIR, win), BF16),
                        pltpu.VMEM((nq, N_SLABS * PAIR, 1), F32)],
        compiler_params=_params("arbitrary"),
        name="attn_b" if use_bias else "attn_a",
    )(q, k, k, v, v, cache_k, cache_v, extra)


HGRN_SAFE_LOG_DECAY = -60.0
HGRN_GROUP = 128
HGRN_UNROLL = 8


def _hgrn_kernel(c_ref, lbl_ref, cn_ref, s0_ref, o_ref, sfp_ref, sfs_ref,
                 st_ref, qd_ref, kd_ref, v_ref, dec_ref, oi_ref, oin_ref, u_ref, sall_ref,
                 lfp_ref, q_ref, k_ref, kinv_ref, sc_ref, *, layer, depth, p_tiles, n_dec, dec_seq):
    tm = c_ref.shape[0]
    nblk = tm // REC_BLOCK
    i = pl.program_id(0)

    @pl.when(i == 0)
    def _():
        st_ref[...] = jnp.zeros_like(st_ref)

    logit_rows = [lbl_ref[r:r + 1, :] for r in range(depth)]
    mx = functools.reduce(jnp.maximum, logit_rows)
    ex = [jnp.exp(r - mx) for r in logit_rows]
    den = functools.reduce(jnp.add, ex)
    sm = [e / den for e in ex]
    lb_all = functools.reduce(jnp.add, sm[:layer + 1]) - sm[0]
    log_lb = jnp.log(lb_all)
    log_1mlb = jnp.log1p(-lb_all)

    def gates(rows, hd):
        lanes = slice(hd * C_DIM, (hd + 1) * C_DIM)
        z = c_ref[rows, lanes].astype(F32)
        vv = c_ref[rows, C_W + hd * C_DIM:C_W + (hd + 1) * C_DIM].astype(F32)
        qp = c_ref[rows, 2 * C_W + hd * C_DIM:2 * C_W + (hd + 1) * C_DIM].astype(F32)
        q = qp * jax.nn.sigmoid(qp)
        log_sig = -(jnp.maximum(-z, 0.0) + jnp.log(1.0 + jnp.exp(-jnp.abs(z))))
        a = log_lb[:, lanes]
        b = log_1mlb[:, lanes] + log_sig
        lf = jnp.maximum(a, b) + jnp.log(1.0 + jnp.exp(-jnp.abs(a - b)))
        return q, 1.0 - jnp.exp(lf), lf, vv

    grow = lax.broadcasted_iota(jnp.int32, (HGRN_GROUP, HGRN_GROUP), 0)
    gcol = lax.broadcasted_iota(jnp.int32, (HGRN_GROUP, HGRN_GROUP), 1)
    same_block = grow // REC_BLOCK == gcol // REC_BLOCK
    upto = jnp.logical_and(same_block, gcol <= grow)
    after = jnp.logical_and(same_block, gcol > grow)
    sum_upto = jnp.where(upto, 1.0, 0.0).astype(BF16)
    sum_after = jnp.where(after, 1.0, 0.0).astype(BF16)

    def block_sums(ones, parts):
        r = jnp.dot(ones, parts, preferred_element_type=F32)
        return r[:, :C_DIM] + r[:, C_DIM:2 * C_DIM] + r[:, 2 * C_DIM:]

    items = [(hd, slice(g * HGRN_GROUP, (g + 1) * HGRN_GROUP))
             for hd in range(C_HEADS) for g in range(tm // HGRN_GROUP)]
    for hd, rows in items:
        q, kk, lf, vv = gates(rows, hd)
        hi = lf.astype(BF16)
        rest = lf - hi.astype(F32)
        mid = rest.astype(BF16)
        lfp_ref[hd, rows, :] = jnp.concatenate([hi, mid, (rest - mid.astype(F32)).astype(BF16)], axis=1)
        q_ref[hd, rows, :] = q
        k_ref[hd, rows, :] = kk
        v_ref[hd, rows, :] = vv.astype(BF16)
    cum_min = jnp.zeros((HGRN_GROUP, C_DIM), F32)
    for hd, rows in items:
        parts = lfp_ref[hd, rows, :]
        cum = block_sums(sum_upto, parts)
        rem = block_sums(sum_after, parts)
        kk = k_ref[hd, rows, :]
        qd_ref[hd, rows, :] = (q_ref[hd, rows, :] * jnp.exp(cum)).astype(BF16)
        kd_ref[hd, rows, :] = (kk * jnp.exp(rem)).astype(BF16)
        kinv_ref[hd, rows, :] = (kk * jnp.exp(-cum)).astype(BF16)
        dec_ref[hd, rows, :] = jnp.exp(cum + rem)
        cum_min = jnp.minimum(cum_min, cum)
    for hd, rows in items:
        sc = lax.dot_general(qd_ref[hd, rows, :], kinv_ref[hd, rows, :], (((1,), (1,)), ((), ())),
                             preferred_element_type=F32)
        sc_ref[hd, rows, :] = jnp.where(upto, sc, 0.0).astype(BF16)
    for hd, rows in items:
        oin_ref[hd, rows, :] = jnp.dot(sc_ref[hd, rows, :], v_ref[hd, rows, :], preferred_element_type=F32)

    @pl.when(jnp.min(cum_min) <= HGRN_SAFE_LOG_DECAY)
    def _():
        pos = lax.broadcasted_iota(jnp.int32, (tm, C_DIM), 0) % REC_BLOCK
        for hd in range(C_HEADS):
            q, kk, lf, vv = gates(slice(None), hd)
            cum = lf
            for sft in (1, 2, 4, 8):
                cum = cum + jnp.where(pos >= sft, pltpu.roll(cum, sft, 0), 0.0)
            acc = jnp.zeros((tm, C_DIM), F32)
            for d in range(REC_BLOCK):
                kr = kk if d == 0 else pltpu.roll(kk, d, 0)
                cr = cum if d == 0 else pltpu.roll(cum, d, 0)
                vr = vv if d == 0 else pltpu.roll(vv, d, 0)
                w = q * kr * jnp.exp(jnp.minimum(cum - cr, 0.0))
                score = jnp.sum(w, axis=-1, keepdims=True)
                acc = acc + jnp.where(pos >= d, score, 0.0) * vr
            oin_ref[hd] = acc

    def increment(n, carry):
        r0 = pl.multiple_of(n * REC_BLOCK, REC_BLOCK)
        for hd in range(C_HEADS):
            u_ref[hd, n] = lax.dot_general(v_ref[hd, pl.ds(r0, REC_BLOCK), :], kd_ref[hd, pl.ds(r0, REC_BLOCK), :],
                                           (((0,), (0,)), ((), ())), preferred_element_type=F32)
        return carry

    lax.fori_loop(0, nblk, increment, 0, unroll=HGRN_UNROLL)

    def advance(st, hd, n, r0):
        sall_ref[hd, n] = st.astype(BF16)
        dec = dec_ref[hd, pl.ds(r0, 8), :]
        return st * jnp.concatenate([dec] * (C_DIM // 8), axis=0) + u_ref[hd, n]

    @pl.when(i < p_tiles)
    def _():
        def step(n, carry):
            r0 = pl.multiple_of(n * REC_BLOCK, REC_BLOCK)
            for hd in range(C_HEADS):
                st_ref[hd] = advance(st_ref[hd], hd, n, r0)
            return carry

        lax.fori_loop(0, nblk, step, 0)

    @pl.when(i == p_tiles - 1)
    def _():
        for hd in range(C_HEADS):
            sfp_ref[0, hd] = st_ref[hd].T

    @pl.when(i == p_tiles)
    def _():
        per_stream = dec_seq // REC_BLOCK
        for b in range(n_dec):
            for hd in range(C_HEADS):
                st = s0_ref[b, hd].T
                for n in range(b * per_stream, (b + 1) * per_stream):
                    st = advance(st, hd, n, n * REC_BLOCK)
                sfs_ref[b, hd] = st.T

    def readout(n, carry):
        r0 = pl.multiple_of(n * REC_BLOCK, REC_BLOCK)
        for hd in range(C_HEADS):
            oi_ref[hd, pl.ds(r0, REC_BLOCK), :] = lax.dot_general(
                qd_ref[hd, pl.ds(r0, REC_BLOCK), :], sall_ref[hd, n],
                (((1,), (1,)), ((), ())), preferred_element_type=F32)
        return carry

    lax.fori_loop(0, nblk, readout, 0, unroll=HGRN_UNROLL)

    for hd in range(C_HEADS):
        lanes = slice(hd * C_DIM, (hd + 1) * C_DIM)
        for g in range(tm // HGRN_GROUP):
            rows = slice(g * HGRN_GROUP, (g + 1) * HGRN_GROUP)
            o = oin_ref[hd, rows, :] + oi_ref[hd, rows, :]
            o = o * lax.rsqrt(jnp.mean(o * o, axis=-1, keepdims=True) + NORM_EPS) * cn_ref[:, lanes]
            og = c_ref[rows, 3 * C_W + hd * C_DIM:3 * C_W + (hd + 1) * C_DIM].astype(F32)
            o_ref[rows, lanes] = (o * (og * jax.nn.sigmoid(og))).astype(o_ref.dtype)


def _hgrn(c, lb_logits, c_norm, s0, *, layer, p_tiles, dec_seq):
    rows = c.shape[0]
    depth, n_dec = s0.shape[:2]
    assert rows == (p_tiles + 1) * TM and n_dec * dec_seq == TM and dec_seq % REC_BLOCK == 0
    nblk = TM // REC_BLOCK
    state = (C_HEADS, C_DIM, C_DIM)
    kern = functools.partial(_hgrn_kernel, layer=layer, depth=depth, p_tiles=p_tiles, n_dec=n_dec, dec_seq=dec_seq)
    return pl.pallas_call(
        kern,
        grid=(p_tiles + 1,),
        in_specs=[pl.BlockSpec((TM, 4 * C_W), lambda i: (i, 0)), _resident(lb_logits.shape),
                  _layer_resident(c_norm, layer), _layer_resident(s0, layer)],
        out_specs=[pl.BlockSpec((TM, C_W), lambda i: (i, 0)),
                   pl.BlockSpec((1,) + state, lambda i: (0, 0, 0, 0)),
                   pl.BlockSpec((n_dec,) + state, lambda i: (0, 0, 0, 0))],
        out_shape=[jax.ShapeDtypeStruct((rows, C_W), BF16),
                   jax.ShapeDtypeStruct((1,) + state, F32),
                   jax.ShapeDtypeStruct((n_dec,) + state, F32)],
        scratch_shapes=[pltpu.VMEM(state, F32),
                        pltpu.VMEM((C_HEADS, TM, C_DIM), BF16),
                        pltpu.VMEM((C_HEADS, TM, C_DIM), BF16),
                        pltpu.VMEM((C_HEADS, TM, C_DIM), BF16),
                        pltpu.VMEM((C_HEADS, TM, C_DIM), F32),
                        pltpu.VMEM((C_HEADS, TM, C_DIM), F32),
                        pltpu.VMEM((C_HEADS, TM, C_DIM), F32),
                        pltpu.VMEM((C_HEADS, nblk, C_DIM, C_DIM), F32),
                        pltpu.VMEM((C_HEADS, nblk, C_DIM, C_DIM), BF16),
                        pltpu.VMEM((C_HEADS, TM, 3 * C_DIM), BF16),
                        pltpu.VMEM((C_HEADS, TM, C_DIM), F32),
                        pltpu.VMEM((C_HEADS, TM, C_DIM), F32),
                        pltpu.VMEM((C_HEADS, TM, C_DIM), BF16),
                        pltpu.VMEM((C_HEADS, TM, C_DIM), BF16)],
        compiler_params=_params("arbitrary"),
        name="hgrn2",
    )(c, lb_logits, c_norm, s0)


def _merge_kernel(*refs, n_x, first_tiles):
    x_refs = refs[:n_x]
    oa_ref, ob_ref, oc_ref, ga_ref, gb_ref, gc_ref, wa_ref, wb_ref, wc_ref, wo_ref, y_ref = refs[n_x:]

    def branch(o_ref, w_ref, g_ref):
        return jax.nn.sigmoid(g_ref[...].astype(F32)) * jnp.dot(o_ref[...], w_ref[...], preferred_element_type=F32)

    mix = branch(oa_ref, wa_ref, ga_ref) + branch(ob_ref, wb_ref, gb_ref) + branch(oc_ref, wc_ref, gc_ref)
    y_ref[...] = _stacked_rows(x_refs, first_tiles) + jnp.dot(mix.astype(BF16), wo_ref[...],
                                                            preferred_element_type=F32)


def _merge(xs, oa, ob, oc, gates, wa, wb, wc, wo, *, layer, tm):
    rows = sum(x.shape[0] for x in xs)
    d_model = wo.shape[-1]
    row_block = lambda width, j=0: pl.BlockSpec((tm, width), lambda i: (i, j))
    x_specs, first_tiles = _stacked_specs(xs, tm)
    return pl.pallas_call(
        functools.partial(_merge_kernel, n_x=len(xs), first_tiles=first_tiles),
        grid=(rows // tm,),
        in_specs=x_specs + [row_block(ATT_W), row_block(ATT_W), row_block(C_W),
                            row_block(d_model, 0), row_block(d_model, 1), row_block(d_model, 2)]
        + [_layer_resident(w, layer) for w in (wa, wb, wc, wo)],
        out_specs=row_block(d_model),
        out_shape=jax.ShapeDtypeStruct((rows, d_model), F32),
        compiler_params=_params("arbitrary"),
        name="merge",
    )(*xs, oa, ob, oc, gates, gates, gates, wa, wb, wc, wo)


FFN_COLS = 256
SUBLANES = 8


def _ffn_kernel(*refs, p_tiles, n_seg, seg_len, d_ff, final_norm, split_out):
    refs = list(refs)
    x_ref, g2_ref, wup_ref, wconv_ref, bconv_ref, wdown_ref, state_ref = refs[:7]
    refs = refs[7:]
    gfin_ref = refs.pop(0) if final_norm else None
    y_refs = [refs.pop(0) for _ in range(2 if split_out else 1)]
    newp_ref, news_ref, act_ref, carry_ref = refs
    cw = FFN_COLS
    n_chunks = d_ff // cw
    tm = n_seg * seg_len
    groups_per_seg = seg_len // SUBLANES
    i = pl.program_id(0)
    is_sample = i == p_tiles

    @pl.when(i == 0)
    def _():
        carry_ref[...] = jnp.zeros_like(carry_ref)

    x = x_ref[...]
    h = _rmsnorm(x, g2_ref[...]).astype(BF16)
    sub = lax.broadcasted_iota(jnp.int32, (SUBLANES, cw), 0)

    def up_proj(c):
        return [jnp.dot(h, wup_ref[:, col0:col0 + cw], preferred_element_type=F32)
                for col0 in (c * cw, d_ff + c * cw)]

    def conv(up, col0):
        cols = slice(col0, col0 + cw)
        w0 = wconv_ref[0:1, cols]
        w1 = wconv_ref[1:2, cols]
        w2 = wconv_ref[2:3, cols]
        bias = bconv_ref[:, cols]
        rolled = lambda g: (pltpu.roll(g, 1, 0), pltpu.roll(g, 2, 0))
        prev = carry_ref[:, cols]
        prev_rolled = None
        out = []
        for gi in range(tm // SUBLANES):
            grp = up[gi * SUBLANES:(gi + 1) * SUBLANES]
            if gi % groups_per_seg == 0:
                sg = gi // groups_per_seg
                st = state_ref[sg, :, cols]
                state_grp = jnp.where(sub == SUBLANES - 2, st[0:1], st[1:2])
                prev_rolled = rolled(jnp.where(is_sample, state_grp, prev))
            cur_rolled = rolled(grp)
            back1 = jnp.where(sub < 1, prev_rolled[0], cur_rolled[0])
            back2 = jnp.where(sub < 2, prev_rolled[1], cur_rolled[1])
            out.append(bias + w0 * back2 + w1 * back1 + w2 * grp)
            prev, prev_rolled = grp, cur_rolled
            if (gi + 1) % groups_per_seg == 0:
                news_ref[gi // groups_per_seg, :, cols] = grp[SUBLANES - 2:]
        carry_ref[:, cols] = prev
        return jnp.concatenate(out, axis=0)

    ups = up_proj(0)
    for c in range(n_chunks):
        nxt = up_proj(c + 1) if c + 1 < n_chunks else None
        u = conv(ups[0], c * cw)
        g = conv(ups[1], d_ff + c * cw)
        act_ref[:, c * cw:(c + 1) * cw] = (u * jax.nn.gelu(g)).astype(BF16)
        ups = nxt
    y = x + jnp.dot(act_ref[...], wdown_ref[...], preferred_element_type=F32)
    if final_norm:
        y = _rmsnorm(y, gfin_ref[...])

    if split_out:
        @pl.when(i < p_tiles)
        def _():
            y_refs[0][...] = y

        @pl.when(is_sample)
        def _():
            y_refs[1][...] = y
    else:
        y_refs[0][...] = y

    @pl.when(i == p_tiles - 1)
    def _():
        newp_ref[0] = carry_ref[SUBLANES - 2:SUBLANES, :]


def _ffn(x, g2, w_up, w_conv, b_conv, w_down, state, g_final, *, layer, p_tiles, dec_seq, split_out):
    rows, d_model = x.shape
    d_ff = w_down.shape[1]
    n_dec = state.shape[1]
    assert d_ff % FFN_COLS == 0 and n_dec * dec_seq == TM and rows == (p_tiles + 1) * TM
    final_norm = g_final is not None
    args = [x, g2, w_up, w_conv, b_conv, w_down, state]
    in_specs = [pl.BlockSpec((TM, d_model), lambda i: (i, 0))] + [_layer_resident(a, layer) for a in args[1:]]
    if final_norm:
        in_specs.append(_resident(g_final.shape))
        args.append(g_final)
    if split_out:
        y_specs = [pl.BlockSpec((TM, d_model), lambda i: (jnp.minimum(i, p_tiles - 1), 0)),
                   pl.BlockSpec((TM, d_model), lambda i: (0, 0))]
        y_shapes = [jax.ShapeDtypeStruct((p_tiles * TM, d_model), F32), jax.ShapeDtypeStruct((TM, d_model), F32)]
    else:
        y_specs = [pl.BlockSpec((TM, d_model), lambda i: (i, 0))]
        y_shapes = [jax.ShapeDtypeStruct((rows, d_model), F32)]
    conv_state = lambda n: (n, CONV_W - 1, 2 * d_ff)
    kern = functools.partial(_ffn_kernel, p_tiles=p_tiles, n_seg=n_dec, seg_len=dec_seq, d_ff=d_ff,
                             final_norm=final_norm, split_out=split_out)
    return pl.pallas_call(
        kern,
        grid=(p_tiles + 1,),
        in_specs=in_specs,
        out_specs=y_specs + [pl.BlockSpec(conv_state(1), lambda i: (0, 0, 0)),
                             pl.BlockSpec(conv_state(n_dec), lambda i: (0, 0, 0))],
        out_shape=y_shapes + [jax.ShapeDtypeStruct(conv_state(1), F32),
                              jax.ShapeDtypeStruct(conv_state(n_dec), F32)],
        scratch_shapes=[pltpu.VMEM((TM, d_ff), BF16),
                        pltpu.VMEM((SUBLANES, 2 * d_ff), F32)],
        compiler_params=_params("arbitrary"),
        name="convffn",
    )(*args)


A_HEAD_ORDER = tuple(h for s in range(A_GROUP) for h in (s, A_GROUP + s))


def _rope_tables(pos):
    half = ROPE_DIM // 2
    inv_freq = ROPE_THETA ** (-jnp.arange(0, ROPE_DIM, 2, dtype=F32) / ROPE_DIM)
    ang = pos.astype(F32)[:, None] * inv_freq[None, :]
    rotated = (jnp.arange(LANES) % HEAD_DIM < ROPE_DIM)[None, :]
    tile = lambda t: jnp.tile(t, (1, LANES // half))
    return jnp.where(rotated, tile(jnp.cos(ang)), 1.0), jnp.where(rotated, tile(jnp.sin(ang)), 0.0)


def _permute_a_heads(a, axis):
    parts = [lax.slice_in_dim(a, h * HEAD_DIM, (h + 1) * HEAD_DIM, axis=axis) for h in A_HEAD_ORDER]
    return jnp.concatenate(parts, axis=axis)


def _prep_w_in(w_in):
    scale = HEAD_DIM ** -0.5
    b0 = ATT_W + 2 * A_KV_W
    return jnp.concatenate([_permute_a_heads(w_in[..., :ATT_W], w_in.ndim - 1) * scale, w_in[..., ATT_W:b0],
                            w_in[..., b0:b0 + ATT_W] * scale, w_in[..., b0 + ATT_W:]], axis=-1).astype(BF16)


def _rel_bias_pairs(table):
    assert CHUNK - 1 <= REL_CLIP
    reach = B_PREV_CHUNKS * CHUNK
    width = reach + CHUNK
    n_clipped = reach + CHUNK - 1 - REL_CLIP
    table = table.astype(F32)
    diag = jnp.concatenate([jnp.broadcast_to(table[:, 2 * REL_CLIP:], (B_HEADS, n_clipped)),
                            table[:, REL_CLIP - (CHUNK - 1):][:, ::-1]], axis=1)
    bias = jnp.stack([diag[:, CHUNK - 1 - q:CHUNK - 1 - q + width] for q in range(CHUNK)], axis=1)
    return bias.reshape(B_HEADS // 2, PAIR, width)


def _sink_pairs(sinks):
    s = jnp.stack([sinks[h] for h in A_HEAD_ORDER]).astype(F32)
    return jnp.broadcast_to(s[:, None], (A_Q_HEADS, CHUNK)).reshape(N_SLABS, PAIR, 1)


def kernel(x_prompt, x_sample, cache_a_k, cache_a_v, cache_b_k, cache_b_v, state_c, state_ffn_conv,
           norm1, w_in, a_sinks, b_rel_bias, c_lb_logits, c_norm, w_branch_a, w_branch_b, w_branch_c,
           w_out, norm2, w_up, w_conv, b_conv, w_down, norm_final):
    batch, seq, d_model = x_prompt.shape
    dec_batch, dec_seq, _ = x_sample.shape
    depth = w_in.shape[0]
    d_ff = w_down.shape[1]
    ra = cache_a_k.shape[2]
    rb = cache_b_k.shape[2]
    n_prompt = batch * seq
    n_sample = dec_batch * dec_seq
    assert batch == 1 and dec_seq == CHUNK and seq % TM == 0 and n_sample == TM
    assert ra == A_PREV_CHUNKS * CHUNK and rb == B_PREV_CHUNKS * CHUNK
    p_tiles = n_prompt // TM

    xs = [x_prompt.reshape(n_prompt, d_model), x_sample.reshape(n_sample, d_model)]
    pos = jnp.concatenate([jnp.arange(seq, dtype=jnp.int32),
                           jnp.tile(PAST_LEN + jnp.arange(dec_seq, dtype=jnp.int32), dec_batch)])
    cos, sin = _rope_tables(pos)

    w_in_k = _prep_w_in(w_in)
    wa_k = _permute_a_heads(w_branch_a, 1).astype(BF16)
    wb_k, wc_k, wo_k = w_branch_b.astype(BF16), w_branch_c.astype(BF16), w_out.astype(BF16)
    w_up_k, w_down_k = w_up.astype(BF16), w_down.astype(BF16)
    norm1_k = norm1.reshape(depth, 1, d_model)
    norm2_k = norm2.reshape(depth, 1, d_model)
    b_conv_k = b_conv.reshape(depth, 1, 2 * d_ff)
    c_norm_k = c_norm.reshape(depth, 1, C_W).astype(F32)
    lb_logits = c_lb_logits.astype(F32)
    ca_k = cache_a_k.reshape(depth, dec_batch * ra, A_KV_W)
    ca_v = cache_a_v.reshape(depth, dec_batch * ra, A_KV_W)
    cb_k = cache_b_k.reshape(depth, dec_batch * rb, ATT_W)
    cb_v = cache_b_v.reshape(depth, dec_batch * rb, ATT_W)
    state_c_k = state_c.astype(F32)

    new_kv = []
    c_states_p, c_states_s, convs_p, convs_s = [], [], [], []
    y_prompt = y_sample = None
    for l in range(depth):
        aq, ak, av, bq, bk, bv, c, gates = _inproj(xs, norm1_k, w_in_k, cos, sin, layer=l, tm=256)
        oa = _attention(aq, ak, av, ca_k, ca_v, _sink_pairs(a_sinks[l]), layer=l, prev=A_PREV_CHUNKS,
                        use_bias=False, use_sink=True, p_tiles=p_tiles)
        ob = _attention(bq, bk, bv, cb_k, cb_v, _rel_bias_pairs(b_rel_bias[l]), layer=l, prev=B_PREV_CHUNKS,
                        use_bias=True, use_sink=False, p_tiles=p_tiles)
        oc, c_state_p, c_state_s = _hgrn(c, lb_logits, c_norm_k, state_c_k, layer=l, p_tiles=p_tiles,
                                         dec_seq=dec_seq)
        x1 = _merge(xs, oa, ob, oc, gates, wa_k, wb_k, wc_k, wo_k, layer=l, tm=TM)
        last = l == depth - 1
        outs = _ffn(x1, norm2_k, w_up_k, w_conv, b_conv_k, w_down_k, state_ffn_conv,
                    norm_final.reshape(1, d_model) if last else None,
                    layer=l, p_tiles=p_tiles, dec_seq=dec_seq, split_out=last)
        if last:
            y_prompt, y_sample, conv_p, conv_s = outs
        else:
            x2, conv_p, conv_s = outs
            xs = [x2]
        new_kv.append((ak, av, bk, bv))
        c_states_p.append(c_state_p)
        c_states_s.append(c_state_s)
        convs_p.append(conv_p)
        convs_s.append(conv_s)

    def new_cache(idx, cache, reach, n_heads):
        prompt = jnp.stack([t[idx][n_prompt - reach:n_prompt] for t in new_kv]).astype(F32)
        fresh = jnp.stack([t[idx][n_prompt:] for t in new_kv]).astype(F32)
        return (prompt.reshape(depth, batch, reach, n_heads, HEAD_DIM),
                jnp.concatenate([cache[:, :, dec_seq:],
                                 fresh.reshape(depth, dec_batch, dec_seq, n_heads, HEAD_DIM)], axis=2))

    a_k_p, a_k_s = new_cache(0, cache_a_k, ra, A_KV_HEADS)
    a_v_p, a_v_s = new_cache(1, cache_a_v, ra, A_KV_HEADS)
    b_k_p, b_k_s = new_cache(2, cache_b_k, rb, B_HEADS)
    b_v_p, b_v_s = new_cache(3, cache_b_v, rb, B_HEADS)
    return (y_prompt.reshape(batch, seq, d_model), y_sample.reshape(dec_batch, dec_seq, d_model),
            a_k_p, a_v_p, b_k_p, b_v_p, jnp.stack(c_states_p), jnp.stack(convs_p),
            a_k_s, a_v_s, b_k_s, b_v_s, jnp.stack(c_states_s), jnp.stack(convs_s))
```

```python
import functools

import jax
import jax.numpy as jnp
from jax import lax
from jax.experimental import pallas as pl
from jax.experimental.pallas import tpu as pltpu

F32 = jnp.float32
BF16 = jnp.bfloat16

CHUNK = 64
HEAD_DIM = 64
LANES = 128
A_Q_HEADS = 8
A_KV_HEADS = 2
A_GROUP = A_Q_HEADS // A_KV_HEADS
A_PREV_CHUNKS = 2
B_HEADS = 8
B_PREV_CHUNKS = 8
REL_CLIP = 128
ROPE_DIM = HEAD_DIM // 4
ROPE_THETA = 500000.0
C_HEADS = 4
C_DIM = 128
REC_BLOCK = 16
CONV_W = 3
NORM_EPS = 1e-6
NEG_INF = -1e30
PAST_LEN = 1024

ATT_W = A_Q_HEADS * HEAD_DIM
A_KV_W = A_KV_HEADS * HEAD_DIM
C_W = C_HEADS * C_DIM
N_SLABS = ATT_W // LANES
PAIR = 2 * CHUNK

TM = 512
VMEM_LIMIT_BYTES = 56 * 1024 * 1024


def _params(*sem):
    return pltpu.CompilerParams(dimension_semantics=sem, vmem_limit_bytes=VMEM_LIMIT_BYTES)


def _resident(shape):
    nd = len(shape)
    return pl.BlockSpec(tuple(shape), lambda *_: (0,) * nd, pipeline_mode=pl.Buffered(1))


def _layer_resident(arr, layer):
    nd = arr.ndim - 1
    return pl.BlockSpec((None,) + tuple(arr.shape[1:]), lambda *_: (layer,) + (0,) * nd,
                        pipeline_mode=pl.Buffered(1))


def _rmsnorm(x, g):
    return x * lax.rsqrt(jnp.mean(x * x, axis=-1, keepdims=True) + NORM_EPS) * g


def _stacked_rows(x_refs, first_tiles):
    if len(x_refs) == 1:
        return x_refs[0][...]
    return jnp.where(pl.program_id(0) < first_tiles, x_refs[0][...], x_refs[1][...])


def _stacked_specs(xs, tm):
    if len(xs) == 1:
        return [pl.BlockSpec((tm, xs[0].shape[1]), lambda i: (i, 0))], 0
    first_tiles = xs[0].shape[0] // tm
    assert xs[0].shape[0] % tm == 0 and xs[1].shape[0] % tm == 0
    return [pl.BlockSpec((tm, xs[0].shape[1]), lambda i: (jnp.minimum(i, first_tiles - 1), 0)),
            pl.BlockSpec((tm, xs[1].shape[1]), lambda i: (jnp.maximum(i - first_tiles, 0), 0))], first_tiles


def _inproj_kernel(*refs, n_x, first_tiles):
    x_refs = refs[:n_x]
    (g_ref, w_ref, wq_ref, cos_ref, sin_ref,
     aq_ref, ak_ref, av_ref, bq_ref, bk_ref, bv_ref, c_ref, gate_ref) = refs[n_x:]
    h = _rmsnorm(_stacked_rows(x_refs, first_tiles), g_ref[...]).astype(BF16)
    cos = cos_ref[...]
    sin = sin_ref[...]
    lane = lax.broadcasted_iota(jnp.int32, cos.shape, 1)
    first_half = (lane % HEAD_DIM) < (ROPE_DIM // 2)

    def proj(lo, hi):
        return jnp.dot(h, w_ref[:, lo:hi], preferred_element_type=F32)

    def rope(p):
        rot = jnp.where(first_half,
                        -pltpu.roll(p, LANES - ROPE_DIM // 2, 1),
                        pltpu.roll(p, ROPE_DIM // 2, 1))
        return p * cos + rot * sin

    p = jnp.dot(h, wq_ref[:, :ATT_W], preferred_element_type=F32)
    for s in range(N_SLABS):
        aq_ref[:, s * LANES:(s + 1) * LANES] = rope(p[:, s * LANES:(s + 1) * LANES]).astype(aq_ref.dtype)
    bq_ref[...] = jnp.dot(h, wq_ref[:, ATT_W:], preferred_element_type=F32).astype(bq_ref.dtype)
    o = ATT_W
    p = proj(o, o + 2 * A_KV_W)
    ak_ref[...] = rope(p[:, :A_KV_W]).astype(ak_ref.dtype)
    av_ref[...] = p[:, A_KV_W:].astype(av_ref.dtype)
    o += 2 * A_KV_W + ATT_W
    for ref in (bk_ref, bv_ref):
        ref[...] = proj(o, o + ATT_W).astype(ref.dtype)
        o += ATT_W
    for s in range(4):
        c_ref[:, s * C_W:(s + 1) * C_W] = proj(o, o + C_W).astype(c_ref.dtype)
        o += C_W
    d_model = w_ref.shape[0]
    for s in range(3):
        gate_ref[:, s * d_model:(s + 1) * d_model] = proj(o, o + d_model).astype(gate_ref.dtype)
        o += d_model


def _inproj(xs, g, w, wq, cos, sin, *, layer, tm):
    rows = sum(x.shape[0] for x in xs)
    _, d_model, in_w = w.shape
    widths = (ATT_W, A_KV_W, A_KV_W, ATT_W, ATT_W, ATT_W, 4 * C_W, 3 * d_model)
    assert sum(widths) == in_w and rows % tm == 0
    row_block = lambda width: pl.BlockSpec((tm, width), lambda i: (i, 0))
    x_specs, first_tiles = _stacked_specs(xs, tm)
    return pl.pallas_call(
        functools.partial(_inproj_kernel, n_x=len(xs), first_tiles=first_tiles),
        grid=(rows // tm,),
        in_specs=x_specs + [_layer_resident(g, layer), _layer_resident(w, layer), _layer_resident(wq, layer),
                            row_block(LANES), row_block(LANES)],
        out_specs=[row_block(wd) for wd in widths],
        out_shape=[jax.ShapeDtypeStruct((rows, wd), BF16) for wd in widths],
        compiler_params=_params("arbitrary"),
        name="inproj",
    )(*xs, g, w, wq, cos, sin)


def _attn_band(q_ref, kf_ref, vf_ref, extra_ref, o_ref, s_ref, p_ref, rd_ref, *,
               nq, win, starts, k_slabs, use_bias, use_sink, n_invalid):
    lower_head = lax.broadcasted_iota(jnp.int32, (CHUNK, LANES), 1) < HEAD_DIM
    col = lax.broadcasted_iota(jnp.int32, (N_SLABS * PAIR, win), 1)
    k_lanes = lambda s: slice((s % k_slabs) * LANES, (s % k_slabs + 1) * LANES)

    for j in range(nq):
        for s in range(N_SLABS):
            q2 = q_ref[j * CHUNK:(j + 1) * CHUNK, s * LANES:(s + 1) * LANES]
            zero = jnp.zeros_like(q2)
            qq = jnp.concatenate([jnp.where(lower_head, q2, zero),
                                  jnp.where(lower_head, zero, q2)], axis=0)
            kw = kf_ref[starts[j]:starts[j] + win, k_lanes(s)]
            s_ref[j, s * PAIR:(s + 1) * PAIR, :] = lax.dot_general(
                qq, kw, (((1,), (1,)), ((), ())), preferred_element_type=F32)

    extra = extra_ref[...].reshape(N_SLABS * PAIR, extra_ref.shape[-1])
    for j in range(nq):
        sc = s_ref[j]
        if use_bias:
            sc = sc + extra
        if n_invalid[j] is not None:
            sc = jnp.where(col < n_invalid[j], NEG_INF, sc)
        m = jnp.max(sc, axis=-1, keepdims=True)
        if use_sink:
            m = jnp.maximum(m, extra)
        e = jnp.exp(sc - m)
        den = jnp.sum(e, axis=-1, keepdims=True)
        if use_sink:
            den = den + jnp.exp(extra - m)
        p_ref[j] = e.astype(BF16)
        rd_ref[j] = 1.0 / den

    for j in range(nq):
        for s in range(N_SLABS):
            vw = vf_ref[starts[j]:starts[j] + win, k_lanes(s)]
            o2 = jnp.dot(p_ref[j, s * PAIR:(s + 1) * PAIR, :], vw, preferred_element_type=F32)
            o2 = o2 * rd_ref[j, s * PAIR:(s + 1) * PAIR, :]
            o_ref[j * CHUNK:(j + 1) * CHUNK, s * LANES:(s + 1) * LANES] = jnp.where(
                lower_head, o2[:CHUNK], o2[CHUNK:]).astype(o_ref.dtype)


def _attn_kernel(q_ref, kp_ref, kc_ref, vp_ref, vc_ref, ck_ref, cv_ref, extra_ref, o_ref,
                 kf_ref, vf_ref, s_ref, p_ref, rd_ref, *, prev, p_tiles, k_slabs, use_bias, use_sink):
    hist = prev * CHUNK
    win = hist + CHUNK
    nq = TM // CHUNK
    i = pl.program_id(0)
    band = functools.partial(_attn_band, q_ref, kf_ref, vf_ref, extra_ref, o_ref, s_ref, p_ref, rd_ref,
                             nq=nq, win=win, k_slabs=k_slabs, use_bias=use_bias, use_sink=use_sink)

    @pl.when(i < p_tiles)
    def _():
        kf_ref[0:hist, :] = kp_ref[...]
        kf_ref[hist:hist + TM, :] = kc_ref[...]
        vf_ref[0:hist, :] = vp_ref[...]
        vf_ref[hist:hist + TM, :] = vc_ref[...]
        n_invalid = [jnp.where(i == 0, (prev - j) * CHUNK, 0) if j < prev else None for j in range(nq)]
        band(starts=[j * CHUNK for j in range(nq)], n_invalid=n_invalid)

    @pl.when(i == p_tiles)
    def _():
        for b in range(nq):
            r0 = b * win
            kf_ref[r0:r0 + hist, :] = ck_ref[b * hist:(b + 1) * hist, :].astype(BF16)
            kf_ref[r0 + hist:r0 + win, :] = kc_ref[b * CHUNK:(b + 1) * CHUNK, :]
            vf_ref[r0:r0 + hist, :] = cv_ref[b * hist:(b + 1) * hist, :].astype(BF16)
            vf_ref[r0 + hist:r0 + win, :] = vc_ref[b * CHUNK:(b + 1) * CHUNK, :]
        band(starts=[b * win for b in range(nq)], n_invalid=[None] * nq)


def _attention(q, k, v, cache_k, cache_v, extra, *, layer, prev, use_bias, use_sink, p_tiles):
    rows = q.shape[0]
    kw = k.shape[1]
    hist = prev * CHUNK
    win = hist + CHUNK
    nq = TM // CHUNK
    assert TM % hist == 0 and rows == (p_tiles + 1) * TM and cache_k.shape[1:] == (nq * hist, kw)
    ratio = TM // hist
    cur = lambda width: pl.BlockSpec((TM, width), lambda i: (i, 0))
    prev_spec = pl.BlockSpec((hist, kw), lambda i: (jnp.maximum(i * ratio - 1, 0), 0))
    kern = functools.partial(_attn_kernel, prev=prev, p_tiles=p_tiles, k_slabs=kw // LANES,
                             use_bias=use_bias, use_sink=use_sink)
    return pl.pallas_call(
        kern,
        grid=(p_tiles + 1,),
        in_specs=[cur(ATT_W), prev_spec, cur(kw), prev_spec, cur(kw),
                  _layer_resident(cache_k, layer), _layer_resident(cache_v, layer), _resident(extra.shape)],
        out_specs=cur(ATT_W),
        out_shape=jax.ShapeDtypeStruct((rows, ATT_W), BF16),
        scratch_shapes=[pltpu.VMEM((nq * win, kw), BF16), pltpu.VMEM((nq * win, kw), BF16),
                        pltpu.VMEM((nq, N_SLABS * PAIR, win), F32),
                        pltpu.VMEM((nq, N_SLABS * PAIR, win), BF16),
                        pltpu.VMEM((nq, N_SLABS * PAIR, 1), F32)],
        compiler_params=_params("arbitrary"),
        name="attn_b" if use_bias else "attn_a",
    )(q, k, k, v, v, cache_k, cache_v, extra)


HGRN_SAFE_LOG_DECAY = -60.0
HGRN_GROUP = 128


def _hgrn_kernel(c_ref, lbl_ref, cn_ref, s0_ref, o_ref, sfp_ref, sfs_ref,
                 st_ref, qd_ref, kd_ref, v_ref, dec_ref, oi_ref, oin_ref, u_ref, sall_ref,
                 lfp_ref, q_ref, k_ref, kinv_ref, sc_ref, vt_ref, *, layer, depth, p_tiles, n_dec, dec_seq):
    tm = c_ref.shape[0]
    i = pl.program_id(0)

    @pl.when(i == 0)
    def _():
        st_ref[...] = jnp.zeros_like(st_ref)

    logit_rows = [lbl_ref[r:r + 1, :] for r in range(depth)]
    mx = functools.reduce(jnp.maximum, logit_rows)
    ex = [jnp.exp(r - mx) for r in logit_rows]
    den = functools.reduce(jnp.add, ex)
    sm = [e / den for e in ex]
    lb_all = functools.reduce(jnp.add, sm[:layer + 1]) - sm[0]
    log_lb = jnp.log(lb_all)
    log_1mlb = jnp.log1p(-lb_all)

    def gates(rows, hd):
        lanes = slice(hd * C_DIM, (hd + 1) * C_DIM)
        z = c_ref[rows, lanes].astype(F32)
        vv = c_ref[rows, C_W + hd * C_DIM:C_W + (hd + 1) * C_DIM].astype(F32)
        qp = c_ref[rows, 2 * C_W + hd * C_DIM:2 * C_W + (hd + 1) * C_DIM].astype(F32)
        q = qp * jax.nn.sigmoid(qp)
        log_sig = -(jnp.maximum(-z, 0.0) + jnp.log(1.0 + jnp.exp(-jnp.abs(z))))
        a = log_lb[:, lanes]
        b = log_1mlb[:, lanes] + log_sig
        lf = jnp.maximum(a, b) + jnp.log(1.0 + jnp.exp(-jnp.abs(a - b)))
        return q, 1.0 - jnp.exp(lf), lf, vv

    grow = lax.broadcasted_iota(jnp.int32, (HGRN_GROUP, HGRN_GROUP), 0)
    gcol = lax.broadcasted_iota(jnp.int32, (HGRN_GROUP, HGRN_GROUP), 1)
    same_block = grow // REC_BLOCK == gcol // REC_BLOCK
    upto = jnp.logical_and(same_block, gcol <= grow)
    after = jnp.logical_and(same_block, gcol > grow)
    sum_upto = jnp.where(upto, 1.0, 0.0).astype(BF16)
    sum_after = jnp.where(after, 1.0, 0.0).astype(BF16)

    def block_sums(ones, parts):
        r = jnp.dot(ones, parts, preferred_element_type=F32)
        return r[:, :C_DIM] + r[:, C_DIM:2 * C_DIM] + r[:, 2 * C_DIM:]

    n_groups = tm // HGRN_GROUP
    items = [(hd, slice(g * HGRN_GROUP, (g + 1) * HGRN_GROUP))
             for hd in range(C_HEADS) for g in range(n_groups)]
    for hd, rows in items:
        q, kk, lf, vv = gates(rows, hd)
        hi = lf.astype(BF16)
        rest = lf - hi.astype(F32)
        mid = rest.astype(BF16)
        lfp_ref[hd, rows, :] = jnp.concatenate([hi, mid, (rest - mid.astype(F32)).astype(BF16)], axis=1)
        q_ref[hd, rows, :] = q
        k_ref[hd, rows, :] = kk
        v_ref[hd, rows, :] = vv.astype(BF16)
        vt_ref[hd, :, rows] = vv.T.astype(BF16)
    cum_min = jnp.zeros((HGRN_GROUP, C_DIM), F32)
    for hd, rows in items:
        parts = lfp_ref[hd, rows, :]
        cum = block_sums(sum_upto, parts)
        rem = block_sums(sum_after, parts)
        kk = k_ref[hd, rows, :]
        qd_ref[hd, rows, :] = (q_ref[hd, rows, :] * jnp.exp(cum)).astype(BF16)
        kd_ref[hd, rows, :] = (kk * jnp.exp(rem)).astype(BF16)
        kinv_ref[hd, rows, :] = (kk * jnp.exp(-cum)).astype(BF16)
        dec_ref[hd, rows, :] = jnp.exp(cum + rem)
        cum_min = jnp.minimum(cum_min, cum)
    for hd, rows in items:
        sc = lax.dot_general(qd_ref[hd, rows, :], kinv_ref[hd, rows, :], (((1,), (1,)), ((), ())),
                             preferred_element_type=F32)
        sc_ref[hd, rows, :] = jnp.where(upto, sc, 0.0).astype(BF16)
    for hd, rows in items:
        oin_ref[hd, rows, :] = jnp.dot(sc_ref[hd, rows, :], v_ref[hd, rows, :], preferred_element_type=F32)

    @pl.when(jnp.min(cum_min) <= HGRN_SAFE_LOG_DECAY)
    def _():
        pos = lax.broadcasted_iota(jnp.int32, (tm, C_DIM), 0) % REC_BLOCK
        for hd in range(C_HEADS):
            q, kk, lf, vv = gates(slice(None), hd)
            cum = lf
            for sft in (1, 2, 4, 8):
                cum = cum + jnp.where(pos >= sft, pltpu.roll(cum, sft, 0), 0.0)
            acc = jnp.zeros((tm, C_DIM), F32)
            for d in range(REC_BLOCK):
                kr = kk if d == 0 else pltpu.roll(kk, d, 0)
                cr = cum if d == 0 else pltpu.roll(cum, d, 0)
                vr = vv if d == 0 else pltpu.roll(vv, d, 0)
                w = q * kr * jnp.exp(jnp.minimum(cum - cr, 0.0))
                score = jnp.sum(w, axis=-1, keepdims=True)
                acc = acc + jnp.where(pos >= d, score, 0.0) * vr
            oin_ref[hd] = acc

    blocks = HGRN_GROUP // REC_BLOCK
    in_block = (lax.broadcasted_iota(jnp.int32, (HGRN_GROUP, blocks * C_DIM), 0) // REC_BLOCK
                == lax.broadcasted_iota(jnp.int32, (HGRN_GROUP, blocks * C_DIM), 1) // C_DIM)

    def block_diagonal(a):
        return jnp.where(in_block, jnp.concatenate([a] * blocks, axis=1), jnp.zeros((), a.dtype))

    for gi, (hd, rows) in enumerate(items):
        u_ref[hd, gi % n_groups] = jnp.dot(vt_ref[hd, :, rows], block_diagonal(kd_ref[hd, rows, :]),
                                           preferred_element_type=F32)

    def advance(st, hd, g, j, r0):
        lanes = slice(j * C_DIM, (j + 1) * C_DIM)
        sall_ref[hd, g, :, lanes] = st.astype(BF16)
        dec = dec_ref[hd, pl.ds(r0, 8), :]
        return st * jnp.concatenate([dec] * (C_DIM // 8), axis=0) + u_ref[hd, g, :, lanes]

    @pl.when(i < p_tiles)
    def _():
        def group_step(g, carry):
            for hd in range(C_HEADS):
                st = st_ref[hd]
                for j in range(blocks):
                    st = advance(st, hd, g, j, pl.multiple_of(g * HGRN_GROUP + j * REC_BLOCK, REC_BLOCK))
                st_ref[hd] = st
            return carry

        lax.fori_loop(0, n_groups, group_step, 0)

    @pl.when(i == p_tiles - 1)
    def _():
        for hd in range(C_HEADS):
            sfp_ref[0, hd] = st_ref[hd].T

    @pl.when(i == p_tiles)
    def _():
        for b in range(n_dec):
            for hd in range(C_HEADS):
                st = s0_ref[b, hd].T
                for n in range(b * dec_seq // REC_BLOCK, (b + 1) * dec_seq // REC_BLOCK):
                    st = advance(st, hd, n // blocks, n % blocks, n * REC_BLOCK)
                sfs_ref[b, hd] = st.T

    for gi, (hd, rows) in enumerate(items):
        o_t = lax.dot_general(sall_ref[hd, gi % n_groups], block_diagonal(qd_ref[hd, rows, :]),
                              (((1,), (1,)), ((), ())), preferred_element_type=F32)
        oi_ref[hd, rows, :] = o_t.T

    for hd in range(C_HEADS):
        lanes = slice(hd * C_DIM, (hd + 1) * C_DIM)
        for g in range(tm // HGRN_GROUP):
            rows = slice(g * HGRN_GROUP, (g + 1) * HGRN_GROUP)
            o = oin_ref[hd, rows, :] + oi_ref[hd, rows, :]
            o = o * lax.rsqrt(jnp.mean(o * o, axis=-1, keepdims=True) + NORM_EPS) * cn_ref[:, lanes]
            og = c_ref[rows, 3 * C_W + hd * C_DIM:3 * C_W + (hd + 1) * C_DIM].astype(F32)
            o_ref[rows, lanes] = (o * (og * jax.nn.sigmoid(og))).astype(o_ref.dtype)


def _hgrn(c, lb_logits, c_norm, s0, *, layer, p_tiles, dec_seq):
    rows = c.shape[0]
    depth, n_dec = s0.shape[:2]
    assert rows == (p_tiles + 1) * TM and n_dec * dec_seq == TM and dec_seq % REC_BLOCK == 0
    assert TM % HGRN_GROUP == 0 and dec_seq * 2 == HGRN_GROUP
    per_group = (C_HEADS, TM // HGRN_GROUP, C_DIM, HGRN_GROUP // REC_BLOCK * C_DIM)
    state = (C_HEADS, C_DIM, C_DIM)
    kern = functools.partial(_hgrn_kernel, layer=layer, depth=depth, p_tiles=p_tiles, n_dec=n_dec, dec_seq=dec_seq)
    return pl.pallas_call(
        kern,
        grid=(p_tiles + 1,),
        in_specs=[pl.BlockSpec((TM, 4 * C_W), lambda i: (i, 0)), _resident(lb_logits.shape),
                  _layer_resident(c_norm, layer), _layer_resident(s0, layer)],
        out_specs=[pl.BlockSpec((TM, C_W), lambda i: (i, 0)),
                   pl.BlockSpec((1,) + state, lambda i: (0, 0, 0, 0)),
                   pl.BlockSpec((n_dec,) + state, lambda i: (0, 0, 0, 0))],
        out_shape=[jax.ShapeDtypeStruct((rows, C_W), BF16),
                   jax.ShapeDtypeStruct((1,) + state, F32),
                   jax.ShapeDtypeStruct((n_dec,) + state, F32)],
        scratch_shapes=[pltpu.VMEM(state, F32),
                        pltpu.VMEM((C_HEADS, TM, C_DIM), BF16),
                        pltpu.VMEM((C_HEADS, TM, C_DIM), BF16),
                        pltpu.VMEM((C_HEADS, TM, C_DIM), BF16),
                        pltpu.VMEM((C_HEADS, TM, C_DIM), F32),
                        pltpu.VMEM((C_HEADS, TM, C_DIM), F32),
                        pltpu.VMEM((C_HEADS, TM, C_DIM), F32),
                        pltpu.VMEM(per_group, F32),
                        pltpu.VMEM(per_group, BF16),
                        pltpu.VMEM((C_HEADS, TM, 3 * C_DIM), BF16),
                        pltpu.VMEM((C_HEADS, TM, C_DIM), F32),
                        pltpu.VMEM((C_HEADS, TM, C_DIM), F32),
                        pltpu.VMEM((C_HEADS, TM, C_DIM), BF16),
                        pltpu.VMEM((C_HEADS, TM, C_DIM), BF16),
                        pltpu.VMEM((C_HEADS, C_DIM, TM), BF16)],
        compiler_params=_params("arbitrary"),
        name="hgrn2",
    )(c, lb_logits, c_norm, s0)


def _merge_kernel(*refs, n_x, first_tiles):
    x_refs = refs[:n_x]
    oa_ref, ob_ref, oc_ref, ga_ref, gb_ref, gc_ref, wa_ref, wb_ref, wc_ref, wo_ref, y_ref = refs[n_x:]

    def branch(o_ref, w_ref, g_ref):
        return jax.nn.sigmoid(g_ref[...].astype(F32)) * jnp.dot(o_ref[...], w_ref[...], preferred_element_type=F32)

    mix = branch(oa_ref, wa_ref, ga_ref) + branch(ob_ref, wb_ref, gb_ref) + branch(oc_ref, wc_ref, gc_ref)
    y_ref[...] = _stacked_rows(x_refs, first_tiles) + jnp.dot(mix.astype(BF16), wo_ref[...],
                                                            preferred_element_type=F32)


def _merge(xs, oa, ob, oc, gates, wa, wb, wc, wo, *, layer, tm):
    rows = sum(x.shape[0] for x in xs)
    d_model = wo.shape[-1]
    row_block = lambda width, j=0: pl.BlockSpec((tm, width), lambda i: (i, j))
    x_specs, first_tiles = _stacked_specs(xs, tm)
    return pl.pallas_call(
        functools.partial(_merge_kernel, n_x=len(xs), first_tiles=first_tiles),
        grid=(rows // tm,),
        in_specs=x_specs + [row_block(ATT_W), row_block(ATT_W), row_block(C_W),
                            row_block(d_model, 0), row_block(d_model, 1), row_block(d_model, 2)]
        + [_layer_resident(w, layer) for w in (wa, wb, wc, wo)],
        out_specs=row_block(d_model),
        out_shape=jax.ShapeDtypeStruct((rows, d_model), F32),
        compiler_params=_params("arbitrary"),
        name="merge",
    )(*xs, oa, ob, oc, gates, gates, gates, wa, wb, wc, wo)


FFN_COLS = 256
SUBLANES = 8


def _ffn_kernel(*refs, p_tiles, n_seg, seg_len, d_ff, final_norm, split_out):
    refs = list(refs)
    x_ref, g2_ref, wup_ref, wconv_ref, bconv_ref, wdown_ref, state_ref = refs[:7]
    refs = refs[7:]
    gfin_ref = refs.pop(0) if final_norm else None
    y_refs = [refs.pop(0) for _ in range(2 if split_out else 1)]
    newp_ref, news_ref, act_ref, carry_ref = refs
    cw = FFN_COLS
    n_chunks = d_ff // cw
    tm = n_seg * seg_len
    groups_per_seg = seg_len // SUBLANES
    i = pl.program_id(0)
    is_sample = i == p_tiles

    @pl.when(i == 0)
    def _():
        carry_ref[...] = jnp.zeros_like(carry_ref)

    x = x_ref[...]
    h = _rmsnorm(x, g2_ref[...]).astype(BF16)
    sub = lax.broadcasted_iota(jnp.int32, (SUBLANES, cw), 0)

    def up_proj(c):
        return [jnp.dot(h, wup_ref[:, col0:col0 + cw], preferred_element_type=F32)
                for col0 in (c * cw, d_ff + c * cw)]

    def conv(up, col0):
        cols = slice(col0, col0 + cw)
        w0 = wconv_ref[0:1, cols]
        w1 = wconv_ref[1:2, cols]
        w2 = wconv_ref[2:3, cols]
        bias = bconv_ref[:, cols]
        rolled = lambda g: (pltpu.roll(g, 1, 0), pltpu.roll(g, 2, 0))
        prev = carry_ref[:, cols]
        prev_rolled = None
        out = []
        for gi in range(tm // SUBLANES):
            grp = up[gi * SUBLANES:(gi + 1) * SUBLANES]
            if gi % groups_per_seg == 0:
                sg = gi // groups_per_seg
                st = state_ref[sg, :, cols]
                state_grp = jnp.where(sub == SUBLANES - 2, st[0:1], st[1:2])
                prev_rolled = rolled(jnp.where(is_sample, state_grp, prev))
            cur_rolled = rolled(grp)
            back1 = jnp.where(sub < 1, prev_rolled[0], cur_rolled[0])
            back2 = jnp.where(sub < 2, prev_rolled[1], cur_rolled[1])
            out.append(bias + w0 * back2 + w1 * back1 + w2 * grp)
            prev, prev_rolled = grp, cur_rolled
            if (gi + 1) % groups_per_seg == 0:
                news_ref[gi // groups_per_seg, :, cols] = grp[SUBLANES - 2:]
        carry_ref[:, cols] = prev
        return jnp.concatenate(out, axis=0)

    ups = up_proj(0)
    for c in range(n_chunks):
        nxt = up_proj(c + 1) if c + 1 < n_chunks else None
        u = conv(ups[0], c * cw)
        g = conv(ups[1], d_ff + c * cw)
        act_ref[:, c * cw:(c + 1) * cw] = (u * jax.nn.gelu(g)).astype(BF16)
        ups = nxt
    y = x + jnp.dot(act_ref[...], wdown_ref[...], preferred_element_type=F32)
    if final_norm:
        y = _rmsnorm(y, gfin_ref[...])

    if split_out:
        @pl.when(i < p_tiles)
        def _():
            y_refs[0][...] = y

        @pl.when(is_sample)
        def _():
            y_refs[1][...] = y
    else:
        y_refs[0][...] = y

    @pl.when(i == p_tiles - 1)
    def _():
        newp_ref[0] = carry_ref[SUBLANES - 2:SUBLANES, :]


def _ffn(x, g2, w_up, w_conv, b_conv, w_down, state, g_final, *, layer, p_tiles, dec_seq, split_out):
    rows, d_model = x.shape
    d_ff = w_down.shape[1]
    n_dec = state.shape[1]
    assert d_ff % FFN_COLS == 0 and n_dec * dec_seq == TM and rows == (p_tiles + 1) * TM
    final_norm = g_final is not None
    args = [x, g2, w_up, w_conv, b_conv, w_down, state]
    in_specs = [pl.BlockSpec((TM, d_model), lambda i: (i, 0))] + [_layer_resident(a, layer) for a in args[1:]]
    if final_norm:
        in_specs.append(_resident(g_final.shape))
        args.append(g_final)
    if split_out:
        y_specs = [pl.BlockSpec((TM, d_model), lambda i: (jnp.minimum(i, p_tiles - 1), 0)),
                   pl.BlockSpec((TM, d_model), lambda i: (0, 0))]
        y_shapes = [jax.ShapeDtypeStruct((p_tiles * TM, d_model), F32), jax.ShapeDtypeStruct((TM, d_model), F32)]
    else:
        y_specs = [pl.BlockSpec((TM, d_model), lambda i: (i, 0))]
        y_shapes = [jax.ShapeDtypeStruct((rows, d_model), F32)]
    conv_state = lambda n: (n, CONV_W - 1, 2 * d_ff)
    kern = functools.partial(_ffn_kernel, p_tiles=p_tiles, n_seg=n_dec, seg_len=dec_seq, d_ff=d_ff,
                             final_norm=final_norm, split_out=split_out)
    return pl.pallas_call(
        kern,
        grid=(p_tiles + 1,),
        in_specs=in_specs,
        out_specs=y_specs + [pl.BlockSpec(conv_state(1), lambda i: (0, 0, 0)),
                             pl.BlockSpec(conv_state(n_dec), lambda i: (0, 0, 0))],
        out_shape=y_shapes + [jax.ShapeDtypeStruct(conv_state(1), F32),
                              jax.ShapeDtypeStruct(conv_state(n_dec), F32)],
        scratch_shapes=[pltpu.VMEM((TM, d_ff), BF16),
                        pltpu.VMEM((SUBLANES, 2 * d_ff), F32)],
        compiler_params=_params("arbitrary"),
        name="convffn",
    )(*args)


A_HEAD_ORDER = tuple(h for s in range(A_GROUP) for h in (s, A_GROUP + s))


def _rope_tables(pos):
    half = ROPE_DIM // 2
    inv_freq = ROPE_THETA ** (-jnp.arange(0, ROPE_DIM, 2, dtype=F32) / ROPE_DIM)
    ang = pos.astype(F32)[:, None] * inv_freq[None, :]
    rotated = (jnp.arange(LANES) % HEAD_DIM < ROPE_DIM)[None, :]
    tile = lambda t: jnp.tile(t, (1, LANES // half))
    return jnp.where(rotated, tile(jnp.cos(ang)), 1.0), jnp.where(rotated, tile(jnp.sin(ang)), 0.0)


def _permute_a_heads(a, axis):
    parts = [lax.slice_in_dim(a, h * HEAD_DIM, (h + 1) * HEAD_DIM, axis=axis) for h in A_HEAD_ORDER]
    return jnp.concatenate(parts, axis=axis)


def _prep_wq(w_in):
    scale = HEAD_DIM ** -0.5
    b0 = ATT_W + 2 * A_KV_W
    return (jnp.concatenate([_permute_a_heads(w_in[..., :ATT_W], w_in.ndim - 1), w_in[..., b0:b0 + ATT_W]],
                            axis=-1) * scale).astype(BF16)


def _rel_bias_pairs(table):
    assert CHUNK - 1 <= REL_CLIP
    reach = B_PREV_CHUNKS * CHUNK
    width = reach + CHUNK
    n_clipped = reach + CHUNK - 1 - REL_CLIP
    table = table.astype(F32)
    diag = jnp.concatenate([jnp.broadcast_to(table[:, 2 * REL_CLIP:], (B_HEADS, n_clipped)),
                            table[:, REL_CLIP - (CHUNK - 1):][:, ::-1]], axis=1)
    period = width + CHUNK
    ring = jnp.concatenate([diag[:, CHUNK - 1:], jnp.zeros((B_HEADS, 2), F32), diag[:, :CHUNK - 1]], axis=1)
    assert ring.shape[1] == period + 1
    bias = jnp.tile(ring, (1, CHUNK))[:, :CHUNK * period].reshape(B_HEADS, CHUNK, period)[:, :, :width]
    return bias.reshape(B_HEADS // 2, PAIR, width)


def _sink_pairs(sinks):
    s = jnp.stack([sinks[h] for h in A_HEAD_ORDER]).astype(F32)
    return jnp.broadcast_to(s[:, None], (A_Q_HEADS, CHUNK)).reshape(N_SLABS, PAIR, 1)


def kernel(x_prompt, x_sample, cache_a_k, cache_a_v, cache_b_k, cache_b_v, state_c, state_ffn_conv,
           norm1, w_in, a_sinks, b_rel_bias, c_lb_logits, c_norm, w_branch_a, w_branch_b, w_branch_c,
           w_out, norm2, w_up, w_conv, b_conv, w_down, norm_final):
    batch, seq, d_model = x_prompt.shape
    dec_batch, dec_seq, _ = x_sample.shape
    depth = w_in.shape[0]
    d_ff = w_down.shape[1]
    ra = cache_a_k.shape[2]
    rb = cache_b_k.shape[2]
    n_prompt = batch * seq
    n_sample = dec_batch * dec_seq
    assert batch == 1 and dec_seq == CHUNK and seq % TM == 0 and n_sample == TM
    assert ra == A_PREV_CHUNKS * CHUNK and rb == B_PREV_CHUNKS * CHUNK
    p_tiles = n_prompt // TM

    xs = [x_prompt.reshape(n_prompt, d_model), x_sample.reshape(n_sample, d_model)]
    pos = jnp.concatenate([jnp.arange(seq, dtype=jnp.int32),
                           jnp.tile(PAST_LEN + jnp.arange(dec_seq, dtype=jnp.int32), dec_batch)])
    cos, sin = _rope_tables(pos)

    w_in_k = w_in.astype(BF16)
    wq_k = _prep_wq(w_in)
    wa_k = _permute_a_heads(w_branch_a, 1).astype(BF16)
    wb_k, wc_k, wo_k = w_branch_b.astype(BF16), w_branch_c.astype(BF16), w_out.astype(BF16)
    w_up_k, w_down_k = w_up.astype(BF16), w_down.astype(BF16)
    norm1_k = norm1.reshape(depth, 1, d_model)
    norm2_k = norm2.reshape(depth, 1, d_model)
    b_conv_k = b_conv.reshape(depth, 1, 2 * d_ff)
    c_norm_k = c_norm.reshape(depth, 1, C_W).astype(F32)
    lb_logits = c_lb_logits.astype(F32)
    ca_k = cache_a_k.reshape(depth, dec_batch * ra, A_KV_W)
    ca_v = cache_a_v.reshape(depth, dec_batch * ra, A_KV_W)
    cb_k = cache_b_k.reshape(depth, dec_batch * rb, ATT_W)
    cb_v = cache_b_v.reshape(depth, dec_batch * rb, ATT_W)
    state_c_k = state_c.astype(F32)

    new_kv = []
    c_states_p, c_states_s, convs_p, convs_s = [], [], [], []
    y_prompt = y_sample = None
    for l in range(depth):
        aq, ak, av, bq, bk, bv, c, gates = _inproj(xs, norm1_k, w_in_k, wq_k, cos, sin, layer=l, tm=256)
        oa = _attention(aq, ak, av, ca_k, ca_v, _sink_pairs(a_sinks[l]), layer=l, prev=A_PREV_CHUNKS,
                        use_bias=False, use_sink=True, p_tiles=p_tiles)
        ob = _attention(bq, bk, bv, cb_k, cb_v, _rel_bias_pairs(b_rel_bias[l]), layer=l, prev=B_PREV_CHUNKS,
                        use_bias=True, use_sink=False, p_tiles=p_tiles)
        oc, c_state_p, c_state_s = _hgrn(c, lb_logits, c_norm_k, state_c_k, layer=l, p_tiles=p_tiles,
                                         dec_seq=dec_seq)
        x1 = _merge(xs, oa, ob, oc, gates, wa_k, wb_k, wc_k, wo_k, layer=l, tm=TM)
        last = l == depth - 1
        outs = _ffn(x1, norm2_k, w_up_k, w_conv, b_conv_k, w_down_k, state_ffn_conv,
                    norm_final.reshape(1, d_model) if last else None,
                    layer=l, p_tiles=p_tiles, dec_seq=dec_seq, split_out=last)
        if last:
            y_prompt, y_sample, conv_p, conv_s = outs
        else:
            x2, conv_p, conv_s = outs
            xs = [x2]
        new_kv.append((ak, av, bk, bv))
        c_states_p.append(c_state_p)
        c_states_s.append(c_state_s)
        convs_p.append(conv_p)
        convs_s.append(conv_s)

    def new_cache(idx, cache, reach, n_heads):
        prompt = jnp.stack([t[idx][n_prompt - reach:n_prompt] for t in new_kv]).astype(F32)
        fresh = jnp.stack([t[idx][n_prompt:] for t in new_kv]).astype(F32)
        shifted = lax.dynamic_update_slice(
            jnp.roll(cache, -dec_seq, axis=2), fresh.reshape(depth, dec_batch, dec_seq, n_heads, HEAD_DIM),
            (0, 0, reach - dec_seq, 0, 0))
        return prompt.reshape(depth, batch, reach, n_heads, HEAD_DIM), shifted

    a_k_p, a_k_s = new_cache(0, cache_a_k, ra, A_KV_HEADS)
    a_v_p, a_v_s = new_cache(1, cache_a_v, ra, A_KV_HEADS)
    b_k_p, b_k_s = new_cache(2, cache_b_k, rb, B_HEADS)
    b_v_p, b_v_s = new_cache(3, cache_b_v, rb, B_HEADS)
    return (y_prompt.reshape(batch, seq, d_model), y_sample.reshape(dec_batch, dec_seq, d_model),
            a_k_p, a_v_p, b_k_p, b_v_p, jnp.stack(c_states_p), jnp.stack(convs_p),
            a_k_s, a_v_s, b_k_s, b_v_s, jnp.stack(c_states_s), jnp.stack(convs_s))
```

```python
import functools

import jax
import jax.numpy as jnp
from jax import lax
from jax.experimental import pallas as pl
from jax.experimental.pallas import tpu as pltpu

F32 = jnp.float32
BF16 = jnp.bfloat16

CHUNK = 64
HEAD_DIM = 64
LANES = 128
A_Q_HEADS = 8
A_KV_HEADS = 2
A_GROUP = A_Q_HEADS // A_KV_HEADS
A_PREV_CHUNKS = 2
B_HEADS = 8
B_PREV_CHUNKS = 8
REL_CLIP = 128
ROPE_DIM = HEAD_DIM // 4
ROPE_THETA = 500000.0
C_HEADS = 4
C_DIM = 128
REC_BLOCK = 16
CONV_W = 3
NORM_EPS = 1e-6
NEG_INF = -1e30
PAST_LEN = 1024

ATT_W = A_Q_HEADS * HEAD_DIM
A_KV_W = A_KV_HEADS * HEAD_DIM
C_W = C_HEADS * C_DIM
N_SLABS = ATT_W // LANES
PAIR = 2 * CHUNK

TM = 512
VMEM_LIMIT_BYTES = 56 * 1024 * 1024


def _params(*sem):
    return pltpu.CompilerParams(dimension_semantics=sem, vmem_limit_bytes=VMEM_LIMIT_BYTES)


def _resident(shape):
    nd = len(shape)
    return pl.BlockSpec(tuple(shape), lambda *_: (0,) * nd, pipeline_mode=pl.Buffered(1))


def _layer_resident(arr, layer):
    nd = arr.ndim - 1
    return pl.BlockSpec((None,) + tuple(arr.shape[1:]), lambda *_: (layer,) + (0,) * nd,
                        pipeline_mode=pl.Buffered(1))


def _rmsnorm(x, g):
    return x * lax.rsqrt(jnp.mean(x * x, axis=-1, keepdims=True) + NORM_EPS) * g


def _stacked_rows(x_refs, first_tiles):
    if len(x_refs) == 1:
        return x_refs[0][...]
    return jnp.where(pl.program_id(0) < first_tiles, x_refs[0][...], x_refs[1][...])


def _stacked_specs(xs, tm):
    if len(xs) == 1:
        return [pl.BlockSpec((tm, xs[0].shape[1]), lambda i: (i, 0))], 0
    first_tiles = xs[0].shape[0] // tm
    assert xs[0].shape[0] % tm == 0 and xs[1].shape[0] % tm == 0
    return [pl.BlockSpec((tm, xs[0].shape[1]), lambda i: (jnp.minimum(i, first_tiles - 1), 0)),
            pl.BlockSpec((tm, xs[1].shape[1]), lambda i: (jnp.maximum(i - first_tiles, 0), 0))], first_tiles


def _inproj_kernel(*refs, n_x, first_tiles):
    x_refs = refs[:n_x]
    (g_ref, w_ref, wq_ref, cos_ref, sin_ref,
     aq_ref, ak_ref, av_ref, bq_ref, bk_ref, bv_ref, c_ref, gate_ref) = refs[n_x:]
    h = _rmsnorm(_stacked_rows(x_refs, first_tiles), g_ref[...]).astype(BF16)
    cos = cos_ref[...]
    sin = sin_ref[...]
    lane = lax.broadcasted_iota(jnp.int32, cos.shape, 1)
    first_half = (lane % HEAD_DIM) < (ROPE_DIM // 2)

    def proj(lo, hi):
        return jnp.dot(h, w_ref[:, lo:hi], preferred_element_type=F32)

    def rope(p):
        rot = jnp.where(first_half,
                        -pltpu.roll(p, LANES - ROPE_DIM // 2, 1),
                        pltpu.roll(p, ROPE_DIM // 2, 1))
        return p * cos + rot * sin

    p = jnp.dot(h, wq_ref[:, :ATT_W], preferred_element_type=F32)
    for s in range(N_SLABS):
        aq_ref[:, s * LANES:(s + 1) * LANES] = rope(p[:, s * LANES:(s + 1) * LANES]).astype(aq_ref.dtype)
    bq_ref[...] = jnp.dot(h, wq_ref[:, ATT_W:], preferred_element_type=F32).astype(bq_ref.dtype)
    o = ATT_W
    p = proj(o, o + 2 * A_KV_W)
    ak_ref[...] = rope(p[:, :A_KV_W]).astype(ak_ref.dtype)
    av_ref[...] = p[:, A_KV_W:].astype(av_ref.dtype)
    o += 2 * A_KV_W + ATT_W
    for ref in (bk_ref, bv_ref):
        ref[...] = proj(o, o + ATT_W).astype(ref.dtype)
        o += ATT_W
    for s in range(4):
        c_ref[:, s * C_W:(s + 1) * C_W] = proj(o, o + C_W).astype(c_ref.dtype)
        o += C_W
    d_model = w_ref.shape[0]
    for s in range(3):
        gate_ref[:, s * d_model:(s + 1) * d_model] = proj(o, o + d_model).astype(gate_ref.dtype)
        o += d_model


def _inproj(xs, g, w, wq, cos, sin, *, layer, tm):
    rows = sum(x.shape[0] for x in xs)
    _, d_model, in_w = w.shape
    widths = (ATT_W, A_KV_W, A_KV_W, ATT_W, ATT_W, ATT_W, 4 * C_W, 3 * d_model)
    assert sum(widths) == in_w and rows % tm == 0
    row_block = lambda width: pl.BlockSpec((tm, width), lambda i: (i, 0))
    x_specs, first_tiles = _stacked_specs(xs, tm)
    return pl.pallas_call(
        functools.partial(_inproj_kernel, n_x=len(xs), first_tiles=first_tiles),
        grid=(rows // tm,),
        in_specs=x_specs + [_layer_resident(g, layer), _layer_resident(w, layer), _layer_resident(wq, layer),
                            row_block(LANES), row_block(LANES)],
        out_specs=[row_block(wd) for wd in widths],
        out_shape=[jax.ShapeDtypeStruct((rows, wd), BF16) for wd in widths],
        compiler_params=_params("arbitrary"),
        name="inproj",
    )(*xs, g, w, wq, cos, sin)


class _Mixer:
    def __init__(self, prev, use_bias, refs):
        (self.q, self.kp, self.kc, self.vp, self.vc, self.ck, self.cv, self.extra, self.o,
         self.kf, self.vf, self.s, self.p, self.rd) = refs
        self.prev = prev
        self.hist = prev * CHUNK
        self.win = self.hist + CHUNK
        self.k_slabs = self.kc.shape[1] // LANES
        self.use_bias = use_bias

    def k_lanes(self, s):
        return slice((s % self.k_slabs) * LANES, (s % self.k_slabs + 1) * LANES)

    def fill_prompt(self):
        self.kf[0:self.hist, :] = self.kp[...]
        self.kf[self.hist:self.hist + TM, :] = self.kc[...]
        self.vf[0:self.hist, :] = self.vp[...]
        self.vf[self.hist:self.hist + TM, :] = self.vc[...]

    def prompt_band(self, full, cache, cur, j, s):
        return full[j * CHUNK:j * CHUNK + self.win, self.k_lanes(s)]

    def sample_band(self, full, cache, cur, b, s):
        return jnp.concatenate([cache[b * self.hist:(b + 1) * self.hist, self.k_lanes(s)],
                                cur[b * CHUNK:(b + 1) * CHUNK, self.k_lanes(s)]], axis=0)

    def scores(self, nq, band):
        lower_head = lax.broadcasted_iota(jnp.int32, (CHUNK, LANES), 1) < HEAD_DIM
        for j in range(nq):
            for s in range(N_SLABS):
                q2 = self.q[j * CHUNK:(j + 1) * CHUNK, s * LANES:(s + 1) * LANES]
                zero = jnp.zeros_like(q2)
                qq = jnp.concatenate([jnp.where(lower_head, q2, zero),
                                      jnp.where(lower_head, zero, q2)], axis=0)
                kw = band(self, self.kf, self.ck, self.kc, j, s)
                self.s[j, s * PAIR:(s + 1) * PAIR, :] = lax.dot_general(
                    qq, kw, (((1,), (1,)), ((), ())), preferred_element_type=F32)

    def softmax(self, nq, n_invalid):
        col = lax.broadcasted_iota(jnp.int32, (N_SLABS * PAIR, self.win), 1)
        extra = self.extra[...].reshape(N_SLABS * PAIR, self.extra.shape[-1])
        for j in range(nq):
            sc = self.s[j]
            if self.use_bias:
                sc = sc + extra
            if n_invalid[j] is not None:
                sc = jnp.where(col < n_invalid[j], NEG_INF, sc)
            m = jnp.max(sc, axis=-1, keepdims=True)
            if not self.use_bias:
                m = jnp.maximum(m, extra)
            e = jnp.exp(sc - m)
            den = jnp.sum(e, axis=-1, keepdims=True)
            if not self.use_bias:
                den = den + jnp.exp(extra - m)
            self.p[j] = e.astype(BF16)
            self.rd[j] = 1.0 / den

    def values(self, nq, band):
        lower_head = lax.broadcasted_iota(jnp.int32, (CHUNK, LANES), 1) < HEAD_DIM
        for j in range(nq):
            for s in range(N_SLABS):
                vw = band(self, self.vf, self.cv, self.vc, j, s)
                o2 = jnp.dot(self.p[j, s * PAIR:(s + 1) * PAIR, :], vw, preferred_element_type=F32)
                o2 = o2 * self.rd[j, s * PAIR:(s + 1) * PAIR, :]
                self.o[j * CHUNK:(j + 1) * CHUNK, s * LANES:(s + 1) * LANES] = jnp.where(
                    lower_head, o2[:CHUNK], o2[CHUNK:]).astype(self.o.dtype)


def _attn_kernel(*refs, p_tiles):
    n_in, n_scr = 8, 5
    ins_a, ins_b = refs[:n_in], refs[n_in:2 * n_in]
    oa_ref, ob_ref = refs[2 * n_in:2 * n_in + 2]
    scr = refs[2 * n_in + 2:]
    mixers = [_Mixer(A_PREV_CHUNKS, False, ins_a + (oa_ref,) + scr[:n_scr]),
              _Mixer(B_PREV_CHUNKS, True, ins_b + (ob_ref,) + scr[n_scr:])]
    nq = TM // CHUNK
    i = pl.program_id(0)

    def run(band, n_invalid):
        for m in mixers:
            m.scores(nq, band)
        for m in mixers:
            m.softmax(nq, n_invalid(m))
        for m in mixers:
            m.values(nq, band)

    @pl.when(i < p_tiles)
    def _():
        for m in mixers:
            m.fill_prompt()
        run(_Mixer.prompt_band,
            lambda m: [jnp.where(i == 0, (m.prev - j) * CHUNK, 0) if j < m.prev else None for j in range(nq)])

    @pl.when(i == p_tiles)
    def _():
        run(_Mixer.sample_band, lambda m: [None] * nq)


def _attention(a, b, *, layer, p_tiles):
    rows = a[0].shape[0]
    nq = TM // CHUNK
    cur = lambda width: pl.BlockSpec((TM, width), lambda i: (i, 0))
    in_specs, args, scratch = [], [], []
    for q, k, v, cache_k, cache_v, extra, prev in (a, b):
        kw = k.shape[1]
        hist = prev * CHUNK
        win = hist + CHUNK
        assert TM % hist == 0 and rows == (p_tiles + 1) * TM and cache_k.shape[1:] == (nq * hist, kw)
        prev_spec = pl.BlockSpec((hist, kw), lambda i, r=TM // hist: (jnp.maximum(i * r - 1, 0), 0))
        in_specs += [cur(ATT_W), prev_spec, cur(kw), prev_spec, cur(kw),
                     _layer_resident(cache_k, layer), _layer_resident(cache_v, layer), _resident(extra.shape)]
        args += [q, k, k, v, v, cache_k, cache_v, extra]
        scratch += [pltpu.VMEM((hist + TM, kw), BF16), pltpu.VMEM((hist + TM, kw), BF16),
                    pltpu.VMEM((nq, N_SLABS * PAIR, win), F32),
                    pltpu.VMEM((nq, N_SLABS * PAIR, win), BF16),
                    pltpu.VMEM((nq, N_SLABS * PAIR, 1), F32)]
    return pl.pallas_call(
        functools.partial(_attn_kernel, p_tiles=p_tiles),
        grid=(p_tiles + 1,),
        in_specs=in_specs,
        out_specs=[cur(ATT_W), cur(ATT_W)],
        out_shape=[jax.ShapeDtypeStruct((rows, ATT_W), BF16)] * 2,
        scratch_shapes=scratch,
        compiler_params=_params("arbitrary"),
        name="band_attn",
    )(*args)


HGRN_SAFE_LOG_DECAY = -60.0
HGRN_GROUP = 128


def _hgrn_kernel(c_ref, lbl_ref, cn_ref, s0_ref, o_ref, sfp_ref, sfs_ref,
                 st_ref, qd_ref, kd_ref, v_ref, dec_ref, oi_ref, oin_ref, u_ref, sall_ref,
                 lfp_ref, q_ref, k_ref, kinv_ref, sc_ref, vt_ref, *, layer, depth, p_tiles, n_dec, dec_seq):
    tm = c_ref.shape[0]
    i = pl.program_id(0)

    @pl.when(i == 0)
    def _():
        st_ref[...] = jnp.zeros_like(st_ref)

    logit_rows = [lbl_ref[r:r + 1, :] for r in range(depth)]
    mx = functools.reduce(jnp.maximum, logit_rows)
    ex = [jnp.exp(r - mx) for r in logit_rows]
    den = functools.reduce(jnp.add, ex)
    sm = [e / den for e in ex]
    lb_all = functools.reduce(jnp.add, sm[:layer + 1]) - sm[0]
    log_lb = jnp.log(lb_all)
    log_1mlb = jnp.log1p(-lb_all)

    def gates(rows, hd):
        lanes = slice(hd * C_DIM, (hd + 1) * C_DIM)
        z = c_ref[rows, lanes].astype(F32)
        vv = c_ref[rows, C_W + hd * C_DIM:C_W + (hd + 1) * C_DIM].astype(F32)
        qp = c_ref[rows, 2 * C_W + hd * C_DIM:2 * C_W + (hd + 1) * C_DIM].astype(F32)
        q = qp * jax.nn.sigmoid(qp)
        log_sig = -(jnp.maximum(-z, 0.0) + jnp.log(1.0 + jnp.exp(-jnp.abs(z))))
        a = log_lb[:, lanes]
        b = log_1mlb[:, lanes] + log_sig
        lf = jnp.maximum(a, b) + jnp.log(1.0 + jnp.exp(-jnp.abs(a - b)))
        return q, 1.0 - jnp.exp(lf), lf, vv

    grow = lax.broadcasted_iota(jnp.int32, (HGRN_GROUP, HGRN_GROUP), 0)
    gcol = lax.broadcasted_iota(jnp.int32, (HGRN_GROUP, HGRN_GROUP), 1)
    same_block = grow // REC_BLOCK == gcol // REC_BLOCK
    upto = jnp.logical_and(same_block, gcol <= grow)
    after = jnp.logical_and(same_block, gcol > grow)
    sum_upto = jnp.where(upto, 1.0, 0.0).astype(BF16)
    sum_after = jnp.where(after, 1.0, 0.0).astype(BF16)

    def block_sums(ones, parts):
        r = jnp.dot(ones, parts, preferred_element_type=F32)
        return r[:, :C_DIM] + r[:, C_DIM:2 * C_DIM] + r[:, 2 * C_DIM:]

    n_groups = tm // HGRN_GROUP
    items = [(hd, slice(g * HGRN_GROUP, (g + 1) * HGRN_GROUP))
             for hd in range(C_HEADS) for g in range(n_groups)]
    for hd, rows in items:
        q, kk, lf, vv = gates(rows, hd)
        hi = lf.astype(BF16)
        rest = lf - hi.astype(F32)
        mid = rest.astype(BF16)
        lfp_ref[hd, rows, :] = jnp.concatenate([hi, mid, (rest - mid.astype(F32)).astype(BF16)], axis=1)
        q_ref[hd, rows, :] = q
        k_ref[hd, rows, :] = kk
        v_ref[hd, rows, :] = vv.astype(BF16)
        vt_ref[hd, :, rows] = vv.T.astype(BF16)
    cum_min = jnp.zeros((HGRN_GROUP, C_DIM), F32)
    for hd, rows in items:
        parts = lfp_ref[hd, rows, :]
        cum = block_sums(sum_upto, parts)
        rem = block_sums(sum_after, parts)
        kk = k_ref[hd, rows, :]
        qd_ref[hd, rows, :] = (q_ref[hd, rows, :] * jnp.exp(cum)).astype(BF16)
        kd_ref[hd, rows, :] = (kk * jnp.exp(rem)).astype(BF16)
        kinv_ref[hd, rows, :] = (kk * jnp.exp(-cum)).astype(BF16)
        dec_ref[hd, rows, :] = jnp.exp(cum + rem)
        cum_min = jnp.minimum(cum_min, cum)
    for hd, rows in items:
        sc = lax.dot_general(qd_ref[hd, rows, :], kinv_ref[hd, rows, :], (((1,), (1,)), ((), ())),
                             preferred_element_type=F32)
        sc_ref[hd, rows, :] = jnp.where(upto, sc, 0.0).astype(BF16)
    for hd, rows in items:
        oin_ref[hd, rows, :] = jnp.dot(sc_ref[hd, rows, :], v_ref[hd, rows, :], preferred_element_type=F32)

    @pl.when(jnp.min(cum_min) <= HGRN_SAFE_LOG_DECAY)
    def _():
        pos = lax.broadcasted_iota(jnp.int32, (tm, C_DIM), 0) % REC_BLOCK
        for hd in range(C_HEADS):
            q, kk, lf, vv = gates(slice(None), hd)
            cum = lf
            for sft in (1, 2, 4, 8):
                cum = cum + jnp.where(pos >= sft, pltpu.roll(cum, sft, 0), 0.0)
            acc = jnp.zeros((tm, C_DIM), F32)
            for d in range(REC_BLOCK):
                kr = kk if d == 0 else pltpu.roll(kk, d, 0)
                cr = cum if d == 0 else pltpu.roll(cum, d, 0)
                vr = vv if d == 0 else pltpu.roll(vv, d, 0)
                w = q * kr * jnp.exp(jnp.minimum(cum - cr, 0.0))
                score = jnp.sum(w, axis=-1, keepdims=True)
                acc = acc + jnp.where(pos >= d, score, 0.0) * vr
            oin_ref[hd] = acc

    blocks = HGRN_GROUP // REC_BLOCK
    in_block = (lax.broadcasted_iota(jnp.int32, (HGRN_GROUP, blocks * C_DIM), 0) // REC_BLOCK
                == lax.broadcasted_iota(jnp.int32, (HGRN_GROUP, blocks * C_DIM), 1) // C_DIM)

    def block_diagonal(a):
        return jnp.where(in_block, jnp.concatenate([a] * blocks, axis=1), jnp.zeros((), a.dtype))

    for gi, (hd, rows) in enumerate(items):
        u_ref[hd, gi % n_groups] = jnp.dot(vt_ref[hd, :, rows], block_diagonal(kd_ref[hd, rows, :]),
                                           preferred_element_type=F32)

    def advance(st, hd, g, j, r0):
        lanes = slice(j * C_DIM, (j + 1) * C_DIM)
        sall_ref[hd, g, :, lanes] = st.astype(BF16)
        dec = dec_ref[hd, pl.ds(r0, 8), :]
        return st * jnp.concatenate([dec] * (C_DIM // 8), axis=0) + u_ref[hd, g, :, lanes]

    @pl.when(i < p_tiles)
    def _():
        def group_step(g, carry):
            for hd in range(C_HEADS):
                st = st_ref[hd]
                for j in range(blocks):
                    st = advance(st, hd, g, j, pl.multiple_of(g * HGRN_GROUP + j * REC_BLOCK, REC_BLOCK))
                st_ref[hd] = st
            return carry

        lax.fori_loop(0, n_groups, group_step, 0)

    @pl.when(i == p_tiles - 1)
    def _():
        for hd in range(C_HEADS):
            sfp_ref[0, hd] = st_ref[hd].T

    @pl.when(i == p_tiles)
    def _():
        for b in range(n_dec):
            for hd in range(C_HEADS):
                st = s0_ref[b, hd].T
                for n in range(b * dec_seq // REC_BLOCK, (b + 1) * dec_seq // REC_BLOCK):
                    st = advance(st, hd, n // blocks, n % blocks, n * REC_BLOCK)
                sfs_ref[b, hd] = st.T

    for gi, (hd, rows) in enumerate(items):
        o_t = lax.dot_general(sall_ref[hd, gi % n_groups], block_diagonal(qd_ref[hd, rows, :]),
                              (((1,), (1,)), ((), ())), preferred_element_type=F32)
        oi_ref[hd, rows, :] = o_t.T

    for hd in range(C_HEADS):
        lanes = slice(hd * C_DIM, (hd + 1) * C_DIM)
        for g in range(tm // HGRN_GROUP):
            rows = slice(g * HGRN_GROUP, (g + 1) * HGRN_GROUP)
            o = oin_ref[hd, rows, :] + oi_ref[hd, rows, :]
            o = o * lax.rsqrt(jnp.mean(o * o, axis=-1, keepdims=True) + NORM_EPS) * cn_ref[:, lanes]
            og = c_ref[rows, 3 * C_W + hd * C_DIM:3 * C_W + (hd + 1) * C_DIM].astype(F32)
            o_ref[rows, lanes] = (o * (og * jax.nn.sigmoid(og))).astype(o_ref.dtype)


def _hgrn(c, lb_logits, c_norm, s0, *, layer, p_tiles, dec_seq):
    rows = c.shape[0]
    depth, n_dec = s0.shape[:2]
    assert rows == (p_tiles + 1) * TM and n_dec * dec_seq == TM and dec_seq % REC_BLOCK == 0
    assert TM % HGRN_GROUP == 0 and dec_seq * 2 == HGRN_GROUP
    per_group = (C_HEADS, TM // HGRN_GROUP, C_DIM, HGRN_GROUP // REC_BLOCK * C_DIM)
    state = (C_HEADS, C_DIM, C_DIM)
    kern = functools.partial(_hgrn_kernel, layer=layer, depth=depth, p_tiles=p_tiles, n_dec=n_dec, dec_seq=dec_seq)
    return pl.pallas_call(
        kern,
        grid=(p_tiles + 1,),
        in_specs=[pl.BlockSpec((TM, 4 * C_W), lambda i: (i, 0)), _resident(lb_logits.shape),
                  _layer_resident(c_norm, layer), _layer_resident(s0, layer)],
        out_specs=[pl.BlockSpec((TM, C_W), lambda i: (i, 0)),
                   pl.BlockSpec((1,) + state, lambda i: (0, 0, 0, 0)),
                   pl.BlockSpec((n_dec,) + state, lambda i: (0, 0, 0, 0))],
        out_shape=[jax.ShapeDtypeStruct((rows, C_W), BF16),
                   jax.ShapeDtypeStruct((1,) + state, F32),
                   jax.ShapeDtypeStruct((n_dec,) + state, F32)],
        scratch_shapes=[pltpu.VMEM(state, F32),
                        pltpu.VMEM((C_HEADS, TM, C_DIM), BF16),
                        pltpu.VMEM((C_HEADS, TM, C_DIM), BF16),
                        pltpu.VMEM((C_HEADS, TM, C_DIM), BF16),
                        pltpu.VMEM((C_HEADS, TM, C_DIM), F32),
                        pltpu.VMEM((C_HEADS, TM, C_DIM), F32),
                        pltpu.VMEM((C_HEADS, TM, C_DIM), F32),
                        pltpu.VMEM(per_group, F32),
                        pltpu.VMEM(per_group, BF16),
                        pltpu.VMEM((C_HEADS, TM, 3 * C_DIM), BF16),
                        pltpu.VMEM((C_HEADS, TM, C_DIM), F32),
                        pltpu.VMEM((C_HEADS, TM, C_DIM), F32),
                        pltpu.VMEM((C_HEADS, TM, C_DIM), BF16),
                        pltpu.VMEM((C_HEADS, TM, C_DIM), BF16),
                        pltpu.VMEM((C_HEADS, C_DIM, TM), BF16)],
        compiler_params=_params("arbitrary"),
        name="hgrn2",
    )(c, lb_logits, c_norm, s0)


FFN_COLS = 256
SUBLANES = 8


def _ffn_kernel(*refs, n_x, p_tiles, n_seg, seg_len, d_ff, final_norm, split_out):
    refs = list(refs)
    x_refs = refs[:n_x]
    (oa_ref, ob_ref, oc_ref, ga_ref, gb_ref, gc_ref, wa_ref, wb_ref, wc_ref, wo_ref,
     g2_ref, wup_ref, wconv_ref, bconv_ref, wdown_ref, state_ref) = refs[n_x:n_x + 16]
    refs = refs[n_x + 16:]
    gfin_ref = refs.pop(0) if final_norm else None
    y_refs = [refs.pop(0) for _ in range(2 if split_out else 1)]
    newp_ref, news_ref, act_ref, carry_ref = refs
    cw = FFN_COLS
    n_chunks = d_ff // cw
    tm = n_seg * seg_len
    groups_per_seg = seg_len // SUBLANES
    i = pl.program_id(0)
    is_sample = i == p_tiles

    @pl.when(i == 0)
    def _():
        carry_ref[...] = jnp.zeros_like(carry_ref)

    def branch(o_ref, w_ref, g_ref):
        return jax.nn.sigmoid(g_ref[...].astype(F32)) * jnp.dot(o_ref[...], w_ref[...], preferred_element_type=F32)

    mix = branch(oa_ref, wa_ref, ga_ref) + branch(ob_ref, wb_ref, gb_ref) + branch(oc_ref, wc_ref, gc_ref)
    x = _stacked_rows(x_refs, p_tiles) + jnp.dot(mix.astype(BF16), wo_ref[...], preferred_element_type=F32)
    h = _rmsnorm(x, g2_ref[...]).astype(BF16)
    sub = lax.broadcasted_iota(jnp.int32, (SUBLANES, cw), 0)

    def up_proj(c):
        return [jnp.dot(h, wup_ref[:, col0:col0 + cw], preferred_element_type=F32)
                for col0 in (c * cw, d_ff + c * cw)]

    def conv(up, col0):
        cols = slice(col0, col0 + cw)
        w0 = wconv_ref[0:1, cols]
        w1 = wconv_ref[1:2, cols]
        w2 = wconv_ref[2:3, cols]
        bias = bconv_ref[:, cols]
        rolled = lambda g: (pltpu.roll(g, 1, 0), pltpu.roll(g, 2, 0))
        prev = carry_ref[:, cols]
        prev_rolled = None
        out = []
        for gi in range(tm // SUBLANES):
            grp = up[gi * SUBLANES:(gi + 1) * SUBLANES]
            if gi % groups_per_seg == 0:
                sg = gi // groups_per_seg
                st = state_ref[sg, :, cols]
                state_grp = jnp.where(sub == SUBLANES - 2, st[0:1], st[1:2])
                prev_rolled = rolled(jnp.where(is_sample, state_grp, prev))
            cur_rolled = rolled(grp)
            back1 = jnp.where(sub < 1, prev_rolled[0], cur_rolled[0])
            back2 = jnp.where(sub < 2, prev_rolled[1], cur_rolled[1])
            out.append(bias + w0 * back2 + w1 * back1 + w2 * grp)
            prev, prev_rolled = grp, cur_rolled
            if (gi + 1) % groups_per_seg == 0:
                news_ref[gi // groups_per_seg, :, cols] = grp[SUBLANES - 2:]
        carry_ref[:, cols] = prev
        return jnp.concatenate(out, axis=0)

    ups = up_proj(0)
    for c in range(n_chunks):
        nxt = up_proj(c + 1) if c + 1 < n_chunks else None
        u = conv(ups[0], c * cw)
        g = conv(ups[1], d_ff + c * cw)
        act_ref[:, c * cw:(c + 1) * cw] = (u * jax.nn.gelu(g)).astype(BF16)
        ups = nxt
    y = x + jnp.dot(act_ref[...], wdown_ref[...], preferred_element_type=F32)
    if final_norm:
        y = _rmsnorm(y, gfin_ref[...])

    if split_out:
        @pl.when(i < p_tiles)
        def _():
            y_refs[0][...] = y

        @pl.when(is_sample)
        def _():
            y_refs[1][...] = y
    else:
        y_refs[0][...] = y

    @pl.when(i == p_tiles - 1)
    def _():
        newp_ref[0] = carry_ref[SUBLANES - 2:SUBLANES, :]


def _mix_ffn(xs, oa, ob, oc, gates, mix_w, ffn_w, state, g_final, *, layer, p_tiles, dec_seq, split_out):
    rows = sum(x.shape[0] for x in xs)
    d_model = xs[0].shape[1]
    d_ff = ffn_w[4].shape[1]
    n_dec = state.shape[1]
    assert d_ff % FFN_COLS == 0 and n_dec * dec_seq == TM and rows == (p_tiles + 1) * TM
    assert dec_seq % SUBLANES == 0
    final_norm = g_final is not None
    row_block = lambda width, j=0: pl.BlockSpec((TM, width), lambda i: (i, j))
    x_specs, first_tiles = _stacked_specs(xs, TM)
    assert len(xs) == 1 or first_tiles == p_tiles
    resident = list(mix_w) + list(ffn_w) + [state]
    args = list(xs) + [oa, ob, oc, gates, gates, gates] + resident
    in_specs = (x_specs + [row_block(ATT_W), row_block(ATT_W), row_block(C_W),
                           row_block(d_model, 0), row_block(d_model, 1), row_block(d_model, 2)]
                + [_layer_resident(a, layer) for a in resident])
    if final_norm:
        in_specs.append(_resident(g_final.shape))
        args.append(g_final)
    if split_out:
        y_specs = [pl.BlockSpec((TM, d_model), lambda i: (jnp.minimum(i, p_tiles - 1), 0)),
                   pl.BlockSpec((TM, d_model), lambda i: (0, 0))]
        y_shapes = [jax.ShapeDtypeStruct((p_tiles * TM, d_model), F32), jax.ShapeDtypeStruct((TM, d_model), F32)]
    else:
        y_specs = [pl.BlockSpec((TM, d_model), lambda i: (i, 0))]
        y_shapes = [jax.ShapeDtypeStruct((rows, d_model), F32)]
    conv_state = lambda n: (n, CONV_W - 1, 2 * d_ff)
    kern = functools.partial(_ffn_kernel, n_x=len(xs), p_tiles=p_tiles, n_seg=n_dec, seg_len=dec_seq, d_ff=d_ff,
                             final_norm=final_norm, split_out=split_out)
    return pl.pallas_call(
        kern,
        grid=(p_tiles + 1,),
        in_specs=in_specs,
        out_specs=y_specs + [pl.BlockSpec(conv_state(1), lambda i: (0, 0, 0)),
                             pl.BlockSpec(conv_state(n_dec), lambda i: (0, 0, 0))],
        out_shape=y_shapes + [jax.ShapeDtypeStruct(conv_state(1), F32),
                              jax.ShapeDtypeStruct(conv_state(n_dec), F32)],
        scratch_shapes=[pltpu.VMEM((TM, d_ff), BF16),
                        pltpu.VMEM((SUBLANES, 2 * d_ff), F32)],
        compiler_params=_params("arbitrary"),
        name="convffn",
    )(*args)


A_HEAD_ORDER = tuple(h for s in range(A_GROUP) for h in (s, A_GROUP + s))


def _rope_tables(pos):
    half = ROPE_DIM // 2
    inv_freq = ROPE_THETA ** (-jnp.arange(0, ROPE_DIM, 2, dtype=F32) / ROPE_DIM)
    ang = pos.astype(F32)[:, None] * inv_freq[None, :]
    rotated = (jnp.arange(LANES) % HEAD_DIM < ROPE_DIM)[None, :]
    tile = lambda t: jnp.tile(t, (1, LANES // half))
    return jnp.where(rotated, tile(jnp.cos(ang)), 1.0), jnp.where(rotated, tile(jnp.sin(ang)), 0.0)


def _permute_a_heads(a, axis):
    parts = [lax.slice_in_dim(a, h * HEAD_DIM, (h + 1) * HEAD_DIM, axis=axis) for h in A_HEAD_ORDER]
    return jnp.concatenate(parts, axis=axis)


def _prep_wq(w_in):
    scale = HEAD_DIM ** -0.5
    b0 = ATT_W + 2 * A_KV_W
    return (jnp.concatenate([_permute_a_heads(w_in[..., :ATT_W], w_in.ndim - 1), w_in[..., b0:b0 + ATT_W]],
                            axis=-1) * scale).astype(BF16)


def _rel_bias_pairs(table):
    assert CHUNK - 1 <= REL_CLIP
    reach = B_PREV_CHUNKS * CHUNK
    width = reach + CHUNK
    n_clipped = reach + CHUNK - 1 - REL_CLIP
    table = table.astype(F32)
    diag = jnp.concatenate([jnp.broadcast_to(table[:, 2 * REL_CLIP:], (B_HEADS, n_clipped)),
                            table[:, REL_CLIP - (CHUNK - 1):][:, ::-1]], axis=1)
    period = width + CHUNK
    ring = jnp.concatenate([diag[:, CHUNK - 1:], jnp.zeros((B_HEADS, 2), F32), diag[:, :CHUNK - 1]], axis=1)
    assert ring.shape[1] == period + 1
    bias = jnp.tile(ring, (1, CHUNK))[:, :CHUNK * period].reshape(B_HEADS, CHUNK, period)[:, :, :width]
    return bias.reshape(B_HEADS // 2, PAIR, width)


def _sink_pairs(sinks):
    s = jnp.stack([sinks[h] for h in A_HEAD_ORDER]).astype(F32)
    return jnp.broadcast_to(s[:, None], (A_Q_HEADS, CHUNK)).reshape(N_SLABS, PAIR, 1)


def kernel(x_prompt, x_sample, cache_a_k, cache_a_v, cache_b_k, cache_b_v, state_c, state_ffn_conv,
           norm1, w_in, a_sinks, b_rel_bias, c_lb_logits, c_norm, w_branch_a, w_branch_b, w_branch_c,
           w_out, norm2, w_up, w_conv, b_conv, w_down, norm_final):
    batch, seq, d_model = x_prompt.shape
    dec_batch, dec_seq, _ = x_sample.shape
    depth = w_in.shape[0]
    d_ff = w_down.shape[1]
    ra = cache_a_k.shape[2]
    rb = cache_b_k.shape[2]
    n_prompt = batch * seq
    n_sample = dec_batch * dec_seq
    assert batch == 1 and dec_seq == CHUNK and seq % TM == 0 and n_sample == TM
    assert ra == A_PREV_CHUNKS * CHUNK and rb == B_PREV_CHUNKS * CHUNK
    p_tiles = n_prompt // TM

    xs = [x_prompt.reshape(n_prompt, d_model), x_sample.reshape(n_sample, d_model)]
    pos = jnp.concatenate([jnp.arange(seq, dtype=jnp.int32),
                           jnp.tile(PAST_LEN + jnp.arange(dec_seq, dtype=jnp.int32), dec_batch)])
    cos, sin = _rope_tables(pos)

    w_in_k = w_in.astype(BF16)
    wq_k = _prep_wq(w_in)
    wa_k = _permute_a_heads(w_branch_a, 1).astype(BF16)
    wb_k, wc_k, wo_k = w_branch_b.astype(BF16), w_branch_c.astype(BF16), w_out.astype(BF16)
    w_up_k, w_down_k = w_up.astype(BF16), w_down.astype(BF16)
    norm1_k = norm1.reshape(depth, 1, d_model)
    norm2_k = norm2.reshape(depth, 1, d_model)
    b_conv_k = b_conv.reshape(depth, 1, 2 * d_ff)
    c_norm_k = c_norm.reshape(depth, 1, C_W).astype(F32)
    lb_logits = c_lb_logits.astype(F32)
    ca_k = cache_a_k.reshape(depth, dec_batch * ra, A_KV_W).astype(BF16)
    ca_v = cache_a_v.reshape(depth, dec_batch * ra, A_KV_W).astype(BF16)
    cb_k = cache_b_k.reshape(depth, dec_batch * rb, ATT_W).astype(BF16)
    cb_v = cache_b_v.reshape(depth, dec_batch * rb, ATT_W).astype(BF16)
    state_c_k = state_c.astype(F32)

    new_kv = []
    c_states_p, c_states_s, convs_p, convs_s = [], [], [], []
    y_prompt = y_sample = None
    for l in range(depth):
        aq, ak, av, bq, bk, bv, c, gates = _inproj(xs, norm1_k, w_in_k, wq_k, cos, sin, layer=l, tm=TM)
        oa, ob = _attention((aq, ak, av, ca_k, ca_v, _sink_pairs(a_sinks[l]), A_PREV_CHUNKS),
                            (bq, bk, bv, cb_k, cb_v, _rel_bias_pairs(b_rel_bias[l]), B_PREV_CHUNKS),
                            layer=l, p_tiles=p_tiles)
        oc, c_state_p, c_state_s = _hgrn(c, lb_logits, c_norm_k, state_c_k, layer=l, p_tiles=p_tiles,
                                         dec_seq=dec_seq)
        last = l == depth - 1
        outs = _mix_ffn(xs, oa, ob, oc, gates, (wa_k, wb_k, wc_k, wo_k),
                        (norm2_k, w_up_k, w_conv, b_conv_k, w_down_k), state_ffn_conv,
                        norm_final.reshape(1, d_model) if last else None,
                        layer=l, p_tiles=p_tiles, dec_seq=dec_seq, split_out=last)
        if last:
            y_prompt, y_sample, conv_p, conv_s = outs
        else:
            x2, conv_p, conv_s = outs
            xs = [x2]
        new_kv.append((ak, av, bk, bv))
        c_states_p.append(c_state_p)
        c_states_s.append(c_state_s)
        convs_p.append(conv_p)
        convs_s.append(conv_s)

    def new_cache(idx, cache, reach, n_heads):
        prompt = jnp.stack([t[idx][n_prompt - reach:n_prompt] for t in new_kv]).astype(F32)
        fresh = jnp.stack([t[idx][n_prompt:] for t in new_kv]).astype(F32)
        return (prompt.reshape(depth, batch, reach, n_heads, HEAD_DIM),
                jnp.concatenate([cache[:, :, dec_seq:],
                                 fresh.reshape(depth, dec_batch, dec_seq, n_heads, HEAD_DIM)], axis=2))

    a_k_p, a_k_s = new_cache(0, cache_a_k, ra, A_KV_HEADS)
    a_v_p, a_v_s = new_cache(1, cache_a_v, ra, A_KV_HEADS)
    b_k_p, b_k_s = new_cache(2, cache_b_k, rb, B_HEADS)
    b_v_p, b_v_s = new_cache(3, cache_b_v, rb, B_HEADS)
    return (y_prompt.reshape(batch, seq, d_model), y_sample.reshape(dec_batch, dec_seq, d_model),
            a_k_p, a_v_p, b_k_p, b_v_p, jnp.stack(c_states_p), jnp.stack(convs_p),
            a_k_s, a_v_s, b_k_s, b_v_s, jnp.stack(c_states_s), jnp.stack(convs_s))
```

```python
import functools

import jax
import jax.numpy as jnp
from jax import lax
from jax.experimental import pallas as pl
from jax.experimental.pallas import tpu as pltpu

F32 = jnp.float32
BF16 = jnp.bfloat16

CHUNK = 64
HEAD_DIM = 64
LANES = 128
A_Q_HEADS = 8
A_KV_HEADS = 2
A_GROUP = A_Q_HEADS // A_KV_HEADS
A_PREV_CHUNKS = 2
B_HEADS = 8
B_PREV_CHUNKS = 8
REL_CLIP = 128
ROPE_DIM = HEAD_DIM // 4
ROPE_THETA = 500000.0
C_HEADS = 4
C_DIM = 128
REC_BLOCK = 16
CONV_W = 3
NORM_EPS = 1e-6
NEG_INF = -1e30
PAST_LEN = 1024

ATT_W = A_Q_HEADS * HEAD_DIM
A_KV_W = A_KV_HEADS * HEAD_DIM
C_W = C_HEADS * C_DIM
N_SLABS = ATT_W // LANES
PAIR = 2 * CHUNK

TM = 512
VMEM_LIMIT_BYTES = 56 * 1024 * 1024


def _params(*sem):
    return pltpu.CompilerParams(dimension_semantics=sem, vmem_limit_bytes=VMEM_LIMIT_BYTES)


def _resident(shape):
    nd = len(shape)
    return pl.BlockSpec(tuple(shape), lambda *_: (0,) * nd, pipeline_mode=pl.Buffered(1))


def _layer_resident(arr, layer):
    nd = arr.ndim - 1
    return pl.BlockSpec((None,) + tuple(arr.shape[1:]), lambda *_: (layer,) + (0,) * nd,
                        pipeline_mode=pl.Buffered(1))


def _rmsnorm(x, g):
    return x * lax.rsqrt(jnp.mean(x * x, axis=-1, keepdims=True) + NORM_EPS) * g


def _stacked_rows(x_refs, first_tiles):
    if len(x_refs) == 1:
        return x_refs[0][...]
    return jnp.where(pl.program_id(0) < first_tiles, x_refs[0][...], x_refs[1][...])


def _stacked_specs(xs, tm):
    if len(xs) == 1:
        return [pl.BlockSpec((tm, xs[0].shape[1]), lambda i: (i, 0))], 0
    first_tiles = xs[0].shape[0] // tm
    assert xs[0].shape[0] % tm == 0 and xs[1].shape[0] % tm == 0
    return [pl.BlockSpec((tm, xs[0].shape[1]), lambda i: (jnp.minimum(i, first_tiles - 1), 0)),
            pl.BlockSpec((tm, xs[1].shape[1]), lambda i: (jnp.maximum(i - first_tiles, 0), 0))], first_tiles


def _inproj_kernel(*refs, n_x, first_tiles):
    x_refs = refs[:n_x]
    (g_ref, w_ref, wq_ref, cos_ref, sin_ref,
     aq_ref, ak_ref, av_ref, bq_ref, bk_ref, bv_ref, c_ref, gate_ref) = refs[n_x:]
    h = _rmsnorm(_stacked_rows(x_refs, first_tiles), g_ref[...]).astype(BF16)
    cos = cos_ref[...]
    sin = sin_ref[...]
    lane = lax.broadcasted_iota(jnp.int32, cos.shape, 1)
    first_half = (lane % HEAD_DIM) < (ROPE_DIM // 2)

    def proj(lo, hi):
        return jnp.dot(h, w_ref[:, lo:hi], preferred_element_type=F32)

    def rope(p):
        rot = jnp.where(first_half,
                        -pltpu.roll(p, LANES - ROPE_DIM // 2, 1),
                        pltpu.roll(p, ROPE_DIM // 2, 1))
        return p * cos + rot * sin

    p = jnp.dot(h, wq_ref[:, :ATT_W], preferred_element_type=F32)
    for s in range(N_SLABS):
        aq_ref[:, s * LANES:(s + 1) * LANES] = rope(p[:, s * LANES:(s + 1) * LANES]).astype(aq_ref.dtype)
    bq_ref[...] = jnp.dot(h, wq_ref[:, ATT_W:], preferred_element_type=F32).astype(bq_ref.dtype)
    o = ATT_W
    p = proj(o, o + 2 * A_KV_W)
    ak_ref[...] = rope(p[:, :A_KV_W]).astype(ak_ref.dtype)
    av_ref[...] = p[:, A_KV_W:].astype(av_ref.dtype)
    o += 2 * A_KV_W + ATT_W
    for ref in (bk_ref, bv_ref):
        ref[...] = proj(o, o + ATT_W).astype(ref.dtype)
        o += ATT_W
    for s in range(4):
        c_ref[:, s * C_W:(s + 1) * C_W] = proj(o, o + C_W).astype(c_ref.dtype)
        o += C_W
    d_model = w_ref.shape[0]
    for s in range(3):
        gate_ref[:, s * d_model:(s + 1) * d_model] = proj(o, o + d_model).astype(gate_ref.dtype)
        o += d_model


def _inproj(xs, g, w, wq, cos, sin, *, layer, tm):
    rows = sum(x.shape[0] for x in xs)
    _, d_model, in_w = w.shape
    widths = (ATT_W, A_KV_W, A_KV_W, ATT_W, ATT_W, ATT_W, 4 * C_W, 3 * d_model)
    assert sum(widths) == in_w and rows % tm == 0
    row_block = lambda width: pl.BlockSpec((tm, width), lambda i: (i, 0))
    x_specs, first_tiles = _stacked_specs(xs, tm)
    return pl.pallas_call(
        functools.partial(_inproj_kernel, n_x=len(xs), first_tiles=first_tiles),
        grid=(rows // tm,),
        in_specs=x_specs + [_layer_resident(g, layer), _layer_resident(w, layer), _layer_resident(wq, layer),
                            row_block(LANES), row_block(LANES)],
        out_specs=[row_block(wd) for wd in widths],
        out_shape=[jax.ShapeDtypeStruct((rows, wd), BF16) for wd in widths],
        compiler_params=_params("arbitrary"),
        name="inproj",
    )(*xs, g, w, wq, cos, sin)


class _Mixer:
    def __init__(self, prev, use_bias, refs):
        (self.q, self.kp, self.kc, self.vp, self.vc, self.ck, self.cv, self.extra, self.o,
         self.kf, self.vf, self.s, self.p, self.rd) = refs
        self.prev = prev
        self.hist = prev * CHUNK
        self.win = self.hist + CHUNK
        self.k_slabs = self.kc.shape[1] // LANES
        self.use_bias = use_bias

    def k_lanes(self, s):
        return slice((s % self.k_slabs) * LANES, (s % self.k_slabs + 1) * LANES)

    def fill_prompt(self):
        self.kf[0:self.hist, :] = self.kp[...]
        self.kf[self.hist:self.hist + TM, :] = self.kc[...]
        self.vf[0:self.hist, :] = self.vp[...]
        self.vf[self.hist:self.hist + TM, :] = self.vc[...]

    def prompt_band(self, full, cache, cur, j, s):
        return full[j * CHUNK:j * CHUNK + self.win, self.k_lanes(s)]

    def sample_band(self, full, cache, cur, b, s):
        return jnp.concatenate([cache[b * self.hist:(b + 1) * self.hist, self.k_lanes(s)],
                                cur[b * CHUNK:(b + 1) * CHUNK, self.k_lanes(s)]], axis=0)

    def scores(self, nq, band):
        lower_head = lax.broadcasted_iota(jnp.int32, (CHUNK, LANES), 1) < HEAD_DIM
        for j in range(nq):
            for s in range(N_SLABS):
                q2 = self.q[j * CHUNK:(j + 1) * CHUNK, s * LANES:(s + 1) * LANES]
                zero = jnp.zeros_like(q2)
                qq = jnp.concatenate([jnp.where(lower_head, q2, zero),
                                      jnp.where(lower_head, zero, q2)], axis=0)
                kw = band(self, self.kf, self.ck, self.kc, j, s)
                self.s[j, s * PAIR:(s + 1) * PAIR, :] = lax.dot_general(
                    qq, kw, (((1,), (1,)), ((), ())), preferred_element_type=F32)

    def softmax(self, nq, n_invalid):
        col = lax.broadcasted_iota(jnp.int32, (N_SLABS * PAIR, self.win), 1)
        extra = self.extra[...].reshape(N_SLABS * PAIR, self.extra.shape[-1])
        for j in range(nq):
            sc = self.s[j]
            if self.use_bias:
                sc = sc + extra
            if n_invalid[j] is not None:
                sc = jnp.where(col < n_invalid[j], NEG_INF, sc)
            m = jnp.max(sc, axis=-1, keepdims=True)
            if not self.use_bias:
                m = jnp.maximum(m, extra)
            e = jnp.exp(sc - m)
            den = jnp.sum(e, axis=-1, keepdims=True)
            if not self.use_bias:
                den = den + jnp.exp(extra - m)
            self.p[j] = e.astype(BF16)
            self.rd[j] = 1.0 / den

    def values(self, nq, band):
        lower_head = lax.broadcasted_iota(jnp.int32, (CHUNK, LANES), 1) < HEAD_DIM
        for j in range(nq):
            for s in range(N_SLABS):
                vw = band(self, self.vf, self.cv, self.vc, j, s)
                o2 = jnp.dot(self.p[j, s * PAIR:(s + 1) * PAIR, :], vw, preferred_element_type=F32)
                o2 = o2 * self.rd[j, s * PAIR:(s + 1) * PAIR, :]
                self.o[j * CHUNK:(j + 1) * CHUNK, s * LANES:(s + 1) * LANES] = jnp.where(
                    lower_head, o2[:CHUNK], o2[CHUNK:]).astype(self.o.dtype)


def _attn_kernel(*refs, p_tiles):
    n_in, n_scr = 8, 5
    ins_a, ins_b = refs[:n_in], refs[n_in:2 * n_in]
    oa_ref, ob_ref = refs[2 * n_in:2 * n_in + 2]
    scr = refs[2 * n_in + 2:]
    mixers = [_Mixer(A_PREV_CHUNKS, False, ins_a + (oa_ref,) + scr[:n_scr]),
              _Mixer(B_PREV_CHUNKS, True, ins_b + (ob_ref,) + scr[n_scr:])]
    nq = TM // CHUNK
    i = pl.program_id(0)

    def run(band, n_invalid):
        for m in mixers:
            m.scores(nq, band)
        for m in mixers:
            m.softmax(nq, n_invalid(m))
        for m in mixers:
            m.values(nq, band)

    @pl.when(i < p_tiles)
    def _():
        for m in mixers:
            m.fill_prompt()
        run(_Mixer.prompt_band,
            lambda m: [jnp.where(i == 0, (m.prev - j) * CHUNK, 0) if j < m.prev else None for j in range(nq)])

    @pl.when(i == p_tiles)
    def _():
        run(_Mixer.sample_band, lambda m: [None] * nq)


def _attention(a, b, *, layer, p_tiles):
    rows = a[0].shape[0]
    nq = TM // CHUNK
    cur = lambda width: pl.BlockSpec((TM, width), lambda i: (i, 0))
    in_specs, args, scratch = [], [], []
    for q, k, v, cache_k, cache_v, extra, prev in (a, b):
        kw = k.shape[1]
        hist = prev * CHUNK
        win = hist + CHUNK
        assert TM % hist == 0 and rows == (p_tiles + 1) * TM and cache_k.shape[1:] == (nq * hist, kw)
        prev_spec = pl.BlockSpec((hist, kw), lambda i, r=TM // hist: (jnp.maximum(i * r - 1, 0), 0))
        in_specs += [cur(ATT_W), prev_spec, cur(kw), prev_spec, cur(kw),
                     _layer_resident(cache_k, layer), _layer_resident(cache_v, layer), _resident(extra.shape)]
        args += [q, k, k, v, v, cache_k, cache_v, extra]
        scratch += [pltpu.VMEM((hist + TM, kw), BF16), pltpu.VMEM((hist + TM, kw), BF16),
                    pltpu.VMEM((nq, N_SLABS * PAIR, win), F32),
                    pltpu.VMEM((nq, N_SLABS * PAIR, win), BF16),
                    pltpu.VMEM((nq, N_SLABS * PAIR, 1), F32)]
    return pl.pallas_call(
        functools.partial(_attn_kernel, p_tiles=p_tiles),
        grid=(p_tiles + 1,),
        in_specs=in_specs,
        out_specs=[cur(ATT_W), cur(ATT_W)],
        out_shape=[jax.ShapeDtypeStruct((rows, ATT_W), BF16)] * 2,
        scratch_shapes=scratch,
        compiler_params=_params("arbitrary"),
        name="band_attn",
    )(*args)


HGRN_SAFE_LOG_DECAY = -60.0
HGRN_GROUP = 128


def _hgrn_kernel(c_ref, lbl_ref, cn_ref, s0_ref, o_ref, sfp_ref, sfs_ref,
                 st_ref, qd_ref, kd_ref, v_ref, dec_ref, oi_ref, oin_ref, u_ref, sall_ref,
                 lfp_ref, q_ref, k_ref, kinv_ref, sc_ref, vt_ref, *, layer, depth, p_tiles, n_dec, dec_seq):
    tm = c_ref.shape[0]
    i = pl.program_id(0)
    is_sample = i == p_tiles

    @pl.when(i == 0)
    def _():
        st_ref[...] = jnp.zeros_like(st_ref)

    logit_rows = [lbl_ref[r:r + 1, :] for r in range(depth)]
    mx = functools.reduce(jnp.maximum, logit_rows)
    ex = [jnp.exp(r - mx) for r in logit_rows]
    den = functools.reduce(jnp.add, ex)
    sm = [e / den for e in ex]
    lb_all = functools.reduce(jnp.add, sm[:layer + 1]) - sm[0]
    log_lb = jnp.log(lb_all)
    log_1mlb = jnp.log1p(-lb_all)

    def gates(rows, hd):
        lanes = slice(hd * C_DIM, (hd + 1) * C_DIM)
        z = c_ref[rows, lanes].astype(F32)
        vv = c_ref[rows, C_W + hd * C_DIM:C_W + (hd + 1) * C_DIM].astype(F32)
        qp = c_ref[rows, 2 * C_W + hd * C_DIM:2 * C_W + (hd + 1) * C_DIM].astype(F32)
        q = qp * jax.nn.sigmoid(qp)
        log_sig = -(jnp.maximum(-z, 0.0) + jnp.log(1.0 + jnp.exp(-jnp.abs(z))))
        a = log_lb[:, lanes]
        b = log_1mlb[:, lanes] + log_sig
        lf = jnp.maximum(a, b) + jnp.log(1.0 + jnp.exp(-jnp.abs(a - b)))
        return q, 1.0 - jnp.exp(lf), lf, vv

    grow = lax.broadcasted_iota(jnp.int32, (HGRN_GROUP, HGRN_GROUP), 0)
    gcol = lax.broadcasted_iota(jnp.int32, (HGRN_GROUP, HGRN_GROUP), 1)
    same_block = grow // REC_BLOCK == gcol // REC_BLOCK
    upto = jnp.logical_and(same_block, gcol <= grow)
    after = jnp.logical_and(same_block, gcol > grow)
    sum_upto = jnp.where(upto, 1.0, 0.0).astype(BF16)
    sum_after = jnp.where(after, 1.0, 0.0).astype(BF16)

    def block_sums(ones, parts):
        r = jnp.dot(ones, parts, preferred_element_type=F32)
        return r[:, :C_DIM] + r[:, C_DIM:2 * C_DIM] + r[:, 2 * C_DIM:]

    n_groups = tm // HGRN_GROUP
    items = [(hd, slice(g * HGRN_GROUP, (g + 1) * HGRN_GROUP))
             for hd in range(C_HEADS) for g in range(n_groups)]
    for hd, rows in items:
        q, kk, lf, vv = gates(rows, hd)
        hi = lf.astype(BF16)
        rest = lf - hi.astype(F32)
        mid = rest.astype(BF16)
        lfp_ref[hd, rows, :] = jnp.concatenate([hi, mid, (rest - mid.astype(F32)).astype(BF16)], axis=1)
        q_ref[hd, rows, :] = q
        k_ref[hd, rows, :] = kk
        v_ref[hd, rows, :] = vv.astype(BF16)
        vt_ref[hd, :, rows] = vv.T.astype(BF16)
    cum_min = jnp.zeros((HGRN_GROUP, C_DIM), F32)
    for hd, rows in items:
        parts = lfp_ref[hd, rows, :]
        cum = block_sums(sum_upto, parts)
        rem = block_sums(sum_after, parts)
        kk = k_ref[hd, rows, :]
        qd_ref[hd, rows, :] = (q_ref[hd, rows, :] * jnp.exp(cum)).astype(BF16)
        kd_ref[hd, rows, :] = (kk * jnp.exp(rem)).astype(BF16)
        kinv_ref[hd, rows, :] = (kk * jnp.exp(-cum)).astype(BF16)
        dec_ref[hd, rows, :] = jnp.exp(cum + rem)
        cum_min = jnp.minimum(cum_min, cum)
    for hd, rows in items:
        sc = lax.dot_general(qd_ref[hd, rows, :], kinv_ref[hd, rows, :], (((1,), (1,)), ((), ())),
                             preferred_element_type=F32)
        sc_ref[hd, rows, :] = jnp.where(upto, sc, 0.0).astype(BF16)
    for hd, rows in items:
        oin_ref[hd, rows, :] = jnp.dot(sc_ref[hd, rows, :], v_ref[hd, rows, :], preferred_element_type=F32)

    blocks = HGRN_GROUP // REC_BLOCK
    in_block = (lax.broadcasted_iota(jnp.int32, (HGRN_GROUP, blocks * C_DIM), 0) // REC_BLOCK
                == lax.broadcasted_iota(jnp.int32, (HGRN_GROUP, blocks * C_DIM), 1) // C_DIM)

    def block_diagonal(a):
        return jnp.where(in_block, jnp.concatenate([a] * blocks, axis=1), jnp.zeros((), a.dtype))

    for gi, (hd, rows) in enumerate(items):
        u_ref[hd, gi % n_groups] = jnp.dot(vt_ref[hd, :, rows], block_diagonal(kd_ref[hd, rows, :]),
                                           preferred_element_type=F32)

    per_stream = dec_seq // REC_BLOCK
    for hd in range(C_HEADS):
        st = st_ref[hd]
        for g in range(n_groups):
            rows = slice(g * HGRN_GROUP, (g + 1) * HGRN_GROUP)
            for j in range(blocks):
                n = g * blocks + j
                if n % per_stream == 0:
                    st = jnp.where(is_sample, s0_ref[n // per_stream, hd], st)
                lanes = slice(j * C_DIM, (j + 1) * C_DIM)
                sall_ref[hd, g, :, lanes] = st.astype(BF16)
                dec = dec_ref[hd, n * REC_BLOCK:n * REC_BLOCK + 8, :]
                st = st * jnp.concatenate([dec] * (C_DIM // 8), axis=0) + u_ref[hd, g, :, lanes]
                if (n + 1) % per_stream == 0:
                    sfs_ref[n // per_stream, hd] = st
            o_t = lax.dot_general(sall_ref[hd, g], block_diagonal(qd_ref[hd, rows, :]),
                                  (((1,), (1,)), ((), ())), preferred_element_type=F32)
            oi_ref[hd, rows, :] = o_t.T
        st_ref[hd] = st

    def finish():
        for hd, rows in items:
            lanes = slice(hd * C_DIM, (hd + 1) * C_DIM)
            o = oin_ref[hd, rows, :] + oi_ref[hd, rows, :]
            o = o * lax.rsqrt(jnp.mean(o * o, axis=-1, keepdims=True) + NORM_EPS) * cn_ref[:, lanes]
            og = c_ref[rows, 3 * C_W + hd * C_DIM:3 * C_W + (hd + 1) * C_DIM].astype(F32)
            o_ref[rows, lanes] = (o * (og * jax.nn.sigmoid(og))).astype(o_ref.dtype)

    finish()

    @pl.when(i == p_tiles - 1)
    def _():
        sfp_ref[0] = st_ref[...]

    @pl.when(jnp.min(cum_min) <= HGRN_SAFE_LOG_DECAY)
    def _():
        pos = lax.broadcasted_iota(jnp.int32, (tm, C_DIM), 0) % REC_BLOCK
        for hd in range(C_HEADS):
            q, kk, lf, vv = gates(slice(None), hd)
            cum = lf
            for sft in (1, 2, 4, 8):
                cum = cum + jnp.where(pos >= sft, pltpu.roll(cum, sft, 0), 0.0)
            acc = jnp.zeros((tm, C_DIM), F32)
            for d in range(REC_BLOCK):
                kr = kk if d == 0 else pltpu.roll(kk, d, 0)
                cr = cum if d == 0 else pltpu.roll(cum, d, 0)
                vr = vv if d == 0 else pltpu.roll(vv, d, 0)
                w = q * kr * jnp.exp(jnp.minimum(cum - cr, 0.0))
                score = jnp.sum(w, axis=-1, keepdims=True)
                acc = acc + jnp.where(pos >= d, score, 0.0) * vr
            oin_ref[hd] = acc
        finish()


def _hgrn(c, lb_logits, c_norm, s0, *, layer, p_tiles, dec_seq):
    rows = c.shape[0]
    depth, n_dec = s0.shape[:2]
    assert rows == (p_tiles + 1) * TM and n_dec * dec_seq == TM and dec_seq % REC_BLOCK == 0
    assert TM % HGRN_GROUP == 0 and dec_seq * 2 == HGRN_GROUP
    per_group = (C_HEADS, TM // HGRN_GROUP, C_DIM, HGRN_GROUP // REC_BLOCK * C_DIM)
    state = (C_HEADS, C_DIM, C_DIM)
    kern = functools.partial(_hgrn_kernel, layer=layer, depth=depth, p_tiles=p_tiles, n_dec=n_dec, dec_seq=dec_seq)
    return pl.pallas_call(
        kern,
        grid=(p_tiles + 1,),
        in_specs=[pl.BlockSpec((TM, 4 * C_W), lambda i: (i, 0)), _resident(lb_logits.shape),
                  _layer_resident(c_norm, layer), _layer_resident(s0, layer)],
        out_specs=[pl.BlockSpec((TM, C_W), lambda i: (i, 0)),
                   pl.BlockSpec((1,) + state, lambda i: (0, 0, 0, 0)),
                   pl.BlockSpec((n_dec,) + state, lambda i: (0, 0, 0, 0))],
        out_shape=[jax.ShapeDtypeStruct((rows, C_W), BF16),
                   jax.ShapeDtypeStruct((1,) + state, F32),
                   jax.ShapeDtypeStruct((n_dec,) + state, F32)],
        scratch_shapes=[pltpu.VMEM(state, F32),
                        pltpu.VMEM((C_HEADS, TM, C_DIM), BF16),
                        pltpu.VMEM((C_HEADS, TM, C_DIM), BF16),
                        pltpu.VMEM((C_HEADS, TM, C_DIM), BF16),
                        pltpu.VMEM((C_HEADS, TM, C_DIM), F32),
                        pltpu.VMEM((C_HEADS, TM, C_DIM), F32),
                        pltpu.VMEM((C_HEADS, TM, C_DIM), F32),
                        pltpu.VMEM(per_group, F32),
                        pltpu.VMEM(per_group, BF16),
                        pltpu.VMEM((C_HEADS, TM, 3 * C_DIM), BF16),
                        pltpu.VMEM((C_HEADS, TM, C_DIM), F32),
                        pltpu.VMEM((C_HEADS, TM, C_DIM), F32),
                        pltpu.VMEM((C_HEADS, TM, C_DIM), BF16),
                        pltpu.VMEM((C_HEADS, TM, C_DIM), BF16),
                        pltpu.VMEM((C_HEADS, C_DIM, TM), BF16)],
        compiler_params=_params("arbitrary"),
        name="hgrn2",
    )(c, lb_logits, c_norm, s0)


FFN_COLS = 256
SUBLANES = 8


def _ffn_kernel(*refs, n_x, p_tiles, n_seg, seg_len, d_ff, final_norm, split_out):
    refs = list(refs)
    x_refs = refs[:n_x]
    (oa_ref, ob_ref, oc_ref, ga_ref, gb_ref, gc_ref, wa_ref, wb_ref, wc_ref, wo_ref,
     g2_ref, wup_ref, wconv_ref, bconv_ref, wdown_ref, state_ref) = refs[n_x:n_x + 16]
    refs = refs[n_x + 16:]
    gfin_ref = refs.pop(0) if final_norm else None
    y_refs = [refs.pop(0) for _ in range(2 if split_out else 1)]
    newp_ref, news_ref, act_ref, carry_ref = refs
    cw = FFN_COLS
    n_chunks = d_ff // cw
    tm = n_seg * seg_len
    groups_per_seg = seg_len // SUBLANES
    i = pl.program_id(0)
    is_sample = i == p_tiles

    @pl.when(i == 0)
    def _():
        carry_ref[...] = jnp.zeros_like(carry_ref)

    def branch(o_ref, w_ref, g_ref):
        return jax.nn.sigmoid(g_ref[...].astype(F32)) * jnp.dot(o_ref[...], w_ref[...], preferred_element_type=F32)

    mix = branch(oa_ref, wa_ref, ga_ref) + branch(ob_ref, wb_ref, gb_ref) + branch(oc_ref, wc_ref, gc_ref)
    x = _stacked_rows(x_refs, p_tiles) + jnp.dot(mix.astype(BF16), wo_ref[...], preferred_element_type=F32)
    h = _rmsnorm(x, g2_ref[...]).astype(BF16)
    sub = lax.broadcasted_iota(jnp.int32, (SUBLANES, cw), 0)

    def up_proj(c):
        return [jnp.dot(h, wup_ref[:, col0:col0 + cw], preferred_element_type=F32)
                for col0 in (c * cw, d_ff + c * cw)]

    def conv(up, col0):
        cols = slice(col0, col0 + cw)
        w0 = wconv_ref[0:1, cols]
        w1 = wconv_ref[1:2, cols]
        w2 = wconv_ref[2:3, cols]
        bias = bconv_ref[:, cols]
        rolled = lambda g: (pltpu.roll(g, 1, 0), pltpu.roll(g, 2, 0))
        prev = carry_ref[:, cols]
        prev_rolled = None
        out = []
        for gi in range(tm // SUBLANES):
            grp = up[gi * SUBLANES:(gi + 1) * SUBLANES]
            if gi % groups_per_seg == 0:
                sg = gi // groups_per_seg
                st = state_ref[sg, :, cols]
                state_grp = jnp.where(sub == SUBLANES - 2, st[0:1], st[1:2])
                prev_rolled = rolled(jnp.where(is_sample, state_grp, prev))
            cur_rolled = rolled(grp)
            back1 = jnp.where(sub < 1, prev_rolled[0], cur_rolled[0])
            back2 = jnp.where(sub < 2, prev_rolled[1], cur_rolled[1])
            out.append(bias + w0 * back2 + w1 * back1 + w2 * grp)
            prev, prev_rolled = grp, cur_rolled
            if (gi + 1) % groups_per_seg == 0:
                news_ref[gi // groups_per_seg, :, cols] = grp[SUBLANES - 2:]
        carry_ref[:, cols] = prev
        return jnp.concatenate(out, axis=0)

    ups = up_proj(0)
    for c in range(n_chunks):
        nxt = up_proj(c + 1) if c + 1 < n_chunks else None
        u = conv(ups[0], c * cw)
        g = conv(ups[1], d_ff + c * cw)
        act_ref[:, c * cw:(c + 1) * cw] = (u * jax.nn.gelu(g)).astype(BF16)
        ups = nxt
    y = x + jnp.dot(act_ref[...], wdown_ref[...], preferred_element_type=F32)
    if final_norm:
        y = _rmsnorm(y, gfin_ref[...])

    if split_out:
        @pl.when(i < p_tiles)
        def _():
            y_refs[0][...] = y

        @pl.when(is_sample)
        def _():
            y_refs[1][...] = y
    else:
        y_refs[0][...] = y

    @pl.when(i == p_tiles - 1)
    def _():
        newp_ref[0] = carry_ref[SUBLANES - 2:SUBLANES, :]


def _mix_ffn(xs, oa, ob, oc, gates, mix_w, ffn_w, state, g_final, *, layer, p_tiles, dec_seq, split_out):
    rows = sum(x.shape[0] for x in xs)
    d_model = xs[0].shape[1]
    d_ff = ffn_w[4].shape[1]
    n_dec = state.shape[1]
    assert d_ff % FFN_COLS == 0 and n_dec * dec_seq == TM and rows == (p_tiles + 1) * TM
    assert dec_seq % SUBLANES == 0
    final_norm = g_final is not None
    row_block = lambda width, j=0: pl.BlockSpec((TM, width), lambda i: (i, j))
    x_specs, first_tiles = _stacked_specs(xs, TM)
    assert len(xs) == 1 or first_tiles == p_tiles
    resident = list(mix_w) + list(ffn_w) + [state]
    args = list(xs) + [oa, ob, oc, gates, gates, gates] + resident
    in_specs = (x_specs + [row_block(ATT_W), row_block(ATT_W), row_block(C_W),
                           row_block(d_model, 0), row_block(d_model, 1), row_block(d_model, 2)]
                + [_layer_resident(a, layer) for a in resident])
    if final_norm:
        in_specs.append(_resident(g_final.shape))
        args.append(g_final)
    if split_out:
        y_specs = [pl.BlockSpec((TM, d_model), lambda i: (jnp.minimum(i, p_tiles - 1), 0)),
                   pl.BlockSpec((TM, d_model), lambda i: (0, 0))]
        y_shapes = [jax.ShapeDtypeStruct((p_tiles * TM, d_model), F32), jax.ShapeDtypeStruct((TM, d_model), F32)]
    else:
        y_specs = [pl.BlockSpec((TM, d_model), lambda i: (i, 0))]
        y_shapes = [jax.ShapeDtypeStruct((rows, d_model), F32)]
    conv_state = lambda n: (n, CONV_W - 1, 2 * d_ff)
    kern = functools.partial(_ffn_kernel, n_x=len(xs), p_tiles=p_tiles, n_seg=n_dec, seg_len=dec_seq, d_ff=d_ff,
                             final_norm=final_norm, split_out=split_out)
    return pl.pallas_call(
        kern,
        grid=(p_tiles + 1,),
        in_specs=in_specs,
        out_specs=y_specs + [pl.BlockSpec(conv_state(1), lambda i: (0, 0, 0)),
                             pl.BlockSpec(conv_state(n_dec), lambda i: (0, 0, 0))],
        out_shape=y_shapes + [jax.ShapeDtypeStruct(conv_state(1), F32),
                              jax.ShapeDtypeStruct(conv_state(n_dec), F32)],
        scratch_shapes=[pltpu.VMEM((TM, d_ff), BF16),
                        pltpu.VMEM((SUBLANES, 2 * d_ff), F32)],
        compiler_params=_params("arbitrary"),
        name="convffn",
    )(*args)


A_HEAD_ORDER = tuple(h for s in range(A_GROUP) for h in (s, A_GROUP + s))


def _rope_tables(pos):
    half = ROPE_DIM // 2
    inv_freq = ROPE_THETA ** (-jnp.arange(0, ROPE_DIM, 2, dtype=F32) / ROPE_DIM)
    ang = pos.astype(F32)[:, None] * inv_freq[None, :]
    rotated = (jnp.arange(LANES) % HEAD_DIM < ROPE_DIM)[None, :]
    tile = lambda t: jnp.tile(t, (1, LANES // half))
    return jnp.where(rotated, tile(jnp.cos(ang)), 1.0), jnp.where(rotated, tile(jnp.sin(ang)), 0.0)


def _permute_a_heads(a, axis):
    parts = [lax.slice_in_dim(a, h * HEAD_DIM, (h + 1) * HEAD_DIM, axis=axis) for h in A_HEAD_ORDER]
    return jnp.concatenate(parts, axis=axis)


def _prep_wq(w_in):
    scale = HEAD_DIM ** -0.5
    b0 = ATT_W + 2 * A_KV_W
    return (jnp.concatenate([_permute_a_heads(w_in[..., :ATT_W], w_in.ndim - 1), w_in[..., b0:b0 + ATT_W]],
                            axis=-1) * scale).astype(BF16)


def _rel_bias_pairs(table):
    assert CHUNK - 1 <= REL_CLIP
    reach = B_PREV_CHUNKS * CHUNK
    width = reach + CHUNK
    n_clipped = reach + CHUNK - 1 - REL_CLIP
    table = table.astype(F32)
    diag = jnp.concatenate([jnp.broadcast_to(table[:, 2 * REL_CLIP:], (B_HEADS, n_clipped)),
                            table[:, REL_CLIP - (CHUNK - 1):][:, ::-1]], axis=1)
    period = width + CHUNK
    ring = jnp.concatenate([diag[:, CHUNK - 1:], jnp.zeros((B_HEADS, 2), F32), diag[:, :CHUNK - 1]], axis=1)
    assert ring.shape[1] == period + 1
    bias = jnp.tile(ring, (1, CHUNK))[:, :CHUNK * period].reshape(B_HEADS, CHUNK, period)[:, :, :width]
    return bias.reshape(B_HEADS // 2, PAIR, width)


def _sink_pairs(sinks):
    s = jnp.stack([sinks[h] for h in A_HEAD_ORDER]).astype(F32)
    return jnp.broadcast_to(s[:, None], (A_Q_HEADS, CHUNK)).reshape(N_SLABS, PAIR, 1)


def kernel(x_prompt, x_sample, cache_a_k, cache_a_v, cache_b_k, cache_b_v, state_c, state_ffn_conv,
           norm1, w_in, a_sinks, b_rel_bias, c_lb_logits, c_norm, w_branch_a, w_branch_b, w_branch_c,
           w_out, norm2, w_up, w_conv, b_conv, w_down, norm_final):
    batch, seq, d_model = x_prompt.shape
    dec_batch, dec_seq, _ = x_sample.shape
    depth = w_in.shape[0]
    d_ff = w_down.shape[1]
    ra = cache_a_k.shape[2]
    rb = cache_b_k.shape[2]
    n_prompt = batch * seq
    n_sample = dec_batch * dec_seq
    assert batch == 1 and dec_seq == CHUNK and seq % TM == 0 and n_sample == TM
    assert ra == A_PREV_CHUNKS * CHUNK and rb == B_PREV_CHUNKS * CHUNK
    p_tiles = n_prompt // TM

    xs = [x_prompt.reshape(n_prompt, d_model), x_sample.reshape(n_sample, d_model)]
    pos = jnp.concatenate([jnp.arange(seq, dtype=jnp.int32),
                           jnp.tile(PAST_LEN + jnp.arange(dec_seq, dtype=jnp.int32), dec_batch)])
    cos, sin = _rope_tables(pos)

    w_in_k = w_in.astype(BF16)
    wq_k = _prep_wq(w_in)
    wa_k = _permute_a_heads(w_branch_a, 1).astype(BF16)
    wb_k, wc_k, wo_k = w_branch_b.astype(BF16), w_branch_c.astype(BF16), w_out.astype(BF16)
    w_up_k, w_down_k = w_up.astype(BF16), w_down.astype(BF16)
    norm1_k = norm1.reshape(depth, 1, d_model)
    norm2_k = norm2.reshape(depth, 1, d_model)
    b_conv_k = b_conv.reshape(depth, 1, 2 * d_ff)
    c_norm_k = c_norm.reshape(depth, 1, C_W).astype(F32)
    lb_logits = c_lb_logits.astype(F32)
    ca_k = cache_a_k.reshape(depth, dec_batch * ra, A_KV_W).astype(BF16)
    ca_v = cache_a_v.reshape(depth, dec_batch * ra, A_KV_W).astype(BF16)
    cb_k = cache_b_k.reshape(depth, dec_batch * rb, ATT_W).astype(BF16)
    cb_v = cache_b_v.reshape(depth, dec_batch * rb, ATT_W).astype(BF16)
    state_c_k = jnp.swapaxes(state_c.astype(F32), -1, -2)

    new_kv = []
    c_states_p, c_states_s, convs_p, convs_s = [], [], [], []
    y_prompt = y_sample = None
    for l in range(depth):
        aq, ak, av, bq, bk, bv, c, gates = _inproj(xs, norm1_k, w_in_k, wq_k, cos, sin, layer=l, tm=TM)
        oa, ob = _attention((aq, ak, av, ca_k, ca_v, _sink_pairs(a_sinks[l]), A_PREV_CHUNKS),
                            (bq, bk, bv, cb_k, cb_v, _rel_bias_pairs(b_rel_bias[l]), B_PREV_CHUNKS),
                            layer=l, p_tiles=p_tiles)
        oc, c_state_p, c_state_s = _hgrn(c, lb_logits, c_norm_k, state_c_k, layer=l, p_tiles=p_tiles,
                                         dec_seq=dec_seq)
        last = l == depth - 1
        outs = _mix_ffn(xs, oa, ob, oc, gates, (wa_k, wb_k, wc_k, wo_k),
                        (norm2_k, w_up_k, w_conv, b_conv_k, w_down_k), state_ffn_conv,
                        norm_final.reshape(1, d_model) if last else None,
                        layer=l, p_tiles=p_tiles, dec_seq=dec_seq, split_out=last)
        if last:
            y_prompt, y_sample, conv_p, conv_s = outs
        else:
            x2, conv_p, conv_s = outs
            xs = [x2]
        new_kv.append((ak, av, bk, bv))
        c_states_p.append(c_state_p)
        c_states_s.append(c_state_s)
        convs_p.append(conv_p)
        convs_s.append(conv_s)

    def new_cache(idx, cache, reach, n_heads):
        prompt = jnp.stack([t[idx][n_prompt - reach:n_prompt] for t in new_kv]).astype(F32)
        fresh = jnp.stack([t[idx][n_prompt:] for t in new_kv]).astype(F32)
        return (prompt.reshape(depth, batch, reach, n_heads, HEAD_DIM),
                jnp.concatenate([cache[:, :, dec_seq:],
                                 fresh.reshape(depth, dec_batch, dec_seq, n_heads, HEAD_DIM)], axis=2))

    a_k_p, a_k_s = new_cache(0, cache_a_k, ra, A_KV_HEADS)
    a_v_p, a_v_s = new_cache(1, cache_a_v, ra, A_KV_HEADS)
    b_k_p, b_k_s = new_cache(2, cache_b_k, rb, B_HEADS)
    b_v_p, b_v_s = new_cache(3, cache_b_v, rb, B_HEADS)
    return (y_prompt.reshape(batch, seq, d_model), y_sample.reshape(dec_batch, dec_seq, d_model),
            a_k_p, a_v_p, b_k_p, b_v_p, jnp.swapaxes(jnp.stack(c_states_p), -1, -2), jnp.stack(convs_p),
            a_k_s, a_v_s, b_k_s, b_v_s, jnp.swapaxes(jnp.stack(c_states_s), -1, -2), jnp.stack(convs_s))
```

```python
import functools

import jax
import jax.numpy as jnp
from jax import lax
from jax.experimental import pallas as pl
from jax.experimental.pallas import tpu as pltpu

F32 = jnp.float32
BF16 = jnp.bfloat16

CHUNK = 64
HEAD_DIM = 64
LANES = 128
A_Q_HEADS = 8
A_KV_HEADS = 2
A_GROUP = A_Q_HEADS // A_KV_HEADS
A_PREV_CHUNKS = 2
B_HEADS = 8
B_PREV_CHUNKS = 8
REL_CLIP = 128
ROPE_DIM = HEAD_DIM // 4
ROPE_THETA = 500000.0
C_HEADS = 4
C_DIM = 128
REC_BLOCK = 16
CONV_W = 3
NORM_EPS = 1e-6
NEG_INF = -1e30
LOG2_E = 1.4426950408889634
PAST_LEN = 1024

ATT_W = A_Q_HEADS * HEAD_DIM
A_KV_W = A_KV_HEADS * HEAD_DIM
C_W = C_HEADS * C_DIM
N_SLABS = ATT_W // LANES
PAIR = 2 * CHUNK

TM = 512
VMEM_LIMIT_BYTES = 56 * 1024 * 1024


def _params(*sem):
    return pltpu.CompilerParams(dimension_semantics=sem, vmem_limit_bytes=VMEM_LIMIT_BYTES)


def _resident(shape):
    nd = len(shape)
    return pl.BlockSpec(tuple(shape), lambda *_: (0,) * nd, pipeline_mode=pl.Buffered(1))


def _layer_resident(arr, layer):
    nd = arr.ndim - 1
    return pl.BlockSpec((None,) + tuple(arr.shape[1:]), lambda *_: (layer,) + (0,) * nd,
                        pipeline_mode=pl.Buffered(1))


def _rmsnorm(x, g):
    return x * lax.rsqrt(jnp.mean(x * x, axis=-1, keepdims=True) + NORM_EPS) * g


def _stacked_rows(x_refs, first_tiles):
    if len(x_refs) == 1:
        return x_refs[0][...]
    return jnp.where(pl.program_id(0) < first_tiles, x_refs[0][...], x_refs[1][...])


def _stacked_specs(xs, tm):
    if len(xs) == 1:
        return [pl.BlockSpec((tm, xs[0].shape[1]), lambda i: (i, 0))], 0
    first_tiles = xs[0].shape[0] // tm
    assert xs[0].shape[0] % tm == 0 and xs[1].shape[0] % tm == 0
    return [pl.BlockSpec((tm, xs[0].shape[1]), lambda i: (jnp.minimum(i, first_tiles - 1), 0)),
            pl.BlockSpec((tm, xs[1].shape[1]), lambda i: (jnp.maximum(i - first_tiles, 0), 0))], first_tiles


def _inproj_kernel(*refs, n_x, first_tiles):
    x_refs = refs[:n_x]
    (g_ref, w_ref, wq_ref, cos_ref, sin_ref,
     aq_ref, ak_ref, av_ref, bq_ref, bk_ref, bv_ref, c_ref, gate_ref) = refs[n_x:]
    h = _rmsnorm(_stacked_rows(x_refs, first_tiles), g_ref[...]).astype(BF16)
    cos = cos_ref[...]
    sin = sin_ref[...]
    lane = lax.broadcasted_iota(jnp.int32, cos.shape, 1)
    first_half = (lane % HEAD_DIM) < (ROPE_DIM // 2)

    def proj(lo, hi):
        return jnp.dot(h, w_ref[:, lo:hi], preferred_element_type=F32)

    def rope(p):
        rot = jnp.where(first_half,
                        -pltpu.roll(p, LANES - ROPE_DIM // 2, 1),
                        pltpu.roll(p, ROPE_DIM // 2, 1))
        return p * cos + rot * sin

    p = jnp.dot(h, wq_ref[:, :ATT_W], preferred_element_type=F32)
    for s in range(N_SLABS):
        aq_ref[:, s * LANES:(s + 1) * LANES] = rope(p[:, s * LANES:(s + 1) * LANES]).astype(aq_ref.dtype)
    bq_ref[...] = jnp.dot(h, wq_ref[:, ATT_W:], preferred_element_type=F32).astype(bq_ref.dtype)
    o = ATT_W
    p = proj(o, o + 2 * A_KV_W)
    ak_ref[...] = rope(p[:, :A_KV_W]).astype(ak_ref.dtype)
    av_ref[...] = p[:, A_KV_W:].astype(av_ref.dtype)
    o += 2 * A_KV_W + ATT_W
    for ref in (bk_ref, bv_ref):
        ref[...] = proj(o, o + ATT_W).astype(ref.dtype)
        o += ATT_W
    for s in range(4):
        c_ref[:, s * C_W:(s + 1) * C_W] = proj(o, o + C_W).astype(c_ref.dtype)
        o += C_W
    d_model = w_ref.shape[0]
    for s in range(3):
        gate_ref[:, s * d_model:(s + 1) * d_model] = proj(o, o + d_model).astype(gate_ref.dtype)
        o += d_model


def _inproj(xs, g, w, wq, cos, sin, *, layer, tm):
    rows = sum(x.shape[0] for x in xs)
    _, d_model, in_w = w.shape
    widths = (ATT_W, A_KV_W, A_KV_W, ATT_W, ATT_W, ATT_W, 4 * C_W, 3 * d_model)
    assert sum(widths) == in_w and rows % tm == 0
    row_block = lambda width: pl.BlockSpec((tm, width), lambda i: (i, 0))
    x_specs, first_tiles = _stacked_specs(xs, tm)
    return pl.pallas_call(
        functools.partial(_inproj_kernel, n_x=len(xs), first_tiles=first_tiles),
        grid=(rows // tm,),
        in_specs=x_specs + [_layer_resident(g, layer), _layer_resident(w, layer), _layer_resident(wq, layer),
                            row_block(LANES), row_block(LANES)],
        out_specs=[row_block(wd) for wd in widths],
        out_shape=[jax.ShapeDtypeStruct((rows, wd), BF16) for wd in widths],
        compiler_params=_params("arbitrary"),
        name="inproj",
    )(*xs, g, w, wq, cos, sin)


class _Mixer:
    def __init__(self, prev, use_bias, refs):
        (self.q, self.kp, self.kc, self.vp, self.vc, self.ck, self.cv, self.extra, self.o,
         self.kf, self.vf, self.s, self.p, self.rd) = refs
        self.prev = prev
        self.hist = prev * CHUNK
        self.win = self.hist + CHUNK
        self.k_slabs = self.kc.shape[1] // LANES
        self.use_bias = use_bias

    def k_lanes(self, s):
        return slice((s % self.k_slabs) * LANES, (s % self.k_slabs + 1) * LANES)

    def fill_prompt(self):
        self.kf[0:self.hist, :] = self.kp[...]
        self.kf[self.hist:self.hist + TM, :] = self.kc[...]
        self.vf[0:self.hist, :] = self.vp[...]
        self.vf[self.hist:self.hist + TM, :] = self.vc[...]

    def prompt_band(self, full, cache, cur, j, s):
        return full[j * CHUNK:j * CHUNK + self.win, self.k_lanes(s)]

    def sample_band(self, full, cache, cur, b, s):
        return jnp.concatenate([cache[b * self.hist:(b + 1) * self.hist, self.k_lanes(s)],
                                cur[b * CHUNK:(b + 1) * CHUNK, self.k_lanes(s)]], axis=0)

    def scores(self, nq, band):
        lower_head = lax.broadcasted_iota(jnp.int32, (CHUNK, LANES), 1) < HEAD_DIM
        for j in range(nq):
            for s in range(N_SLABS):
                q2 = self.q[j * CHUNK:(j + 1) * CHUNK, s * LANES:(s + 1) * LANES]
                zero = jnp.zeros_like(q2)
                qq = jnp.concatenate([jnp.where(lower_head, q2, zero),
                                      jnp.where(lower_head, zero, q2)], axis=0)
                kw = band(self, self.kf, self.ck, self.kc, j, s)
                self.s[j, s * PAIR:(s + 1) * PAIR, :] = lax.dot_general(
                    qq, kw, (((1,), (1,)), ((), ())), preferred_element_type=F32)

    def softmax(self, nq, n_invalid):
        col = lax.broadcasted_iota(jnp.int32, (N_SLABS * PAIR, self.win), 1)
        extra = self.extra[...].reshape(N_SLABS * PAIR, self.extra.shape[-1])
        for j in range(nq):
            sc = self.s[j]
            if self.use_bias:
                sc = sc + extra
            if n_invalid[j] is not None:
                sc = jnp.where(col < n_invalid[j], NEG_INF, sc)
            m = jnp.max(sc, axis=-1, keepdims=True)
            if not self.use_bias:
                m = jnp.maximum(m, extra)
            e = jnp.exp2(sc - m)
            den = jnp.sum(e, axis=-1, keepdims=True)
            if not self.use_bias:
                den = den + jnp.exp2(extra - m)
            self.p[j] = e.astype(BF16)
            self.rd[j] = 1.0 / den

    def values(self, nq, band):
        lower_head = lax.broadcasted_iota(jnp.int32, (CHUNK, LANES), 1) < HEAD_DIM
        for j in range(nq):
            for s in range(N_SLABS):
                vw = band(self, self.vf, self.cv, self.vc, j, s)
                o2 = jnp.dot(self.p[j, s * PAIR:(s + 1) * PAIR, :], vw, preferred_element_type=F32)
                o2 = o2 * self.rd[j, s * PAIR:(s + 1) * PAIR, :]
                self.o[j * CHUNK:(j + 1) * CHUNK, s * LANES:(s + 1) * LANES] = jnp.where(
                    lower_head, o2[:CHUNK], o2[CHUNK:]).astype(self.o.dtype)


def _attn_kernel(*refs, p_tiles):
    n_in, n_scr = 8, 5
    ins_a, ins_b = refs[:n_in], refs[n_in:2 * n_in]
    oa_ref, ob_ref = refs[2 * n_in:2 * n_in + 2]
    scr = refs[2 * n_in + 2:]
    mixers = [_Mixer(A_PREV_CHUNKS, False, ins_a + (oa_ref,) + scr[:n_scr]),
              _Mixer(B_PREV_CHUNKS, True, ins_b + (ob_ref,) + scr[n_scr:])]
    nq = TM // CHUNK
    i = pl.program_id(0)

    def run(band, n_invalid):
        for m in mixers:
            m.scores(nq, band)
        for m in mixers:
            m.softmax(nq, n_invalid(m))
        for m in mixers:
            m.values(nq, band)

    @pl.when(i < p_tiles)
    def _():
        for m in mixers:
            m.fill_prompt()
        run(_Mixer.prompt_band,
            lambda m: [jnp.where(i == 0, (m.prev - j) * CHUNK, 0) if j < m.prev else None for j in range(nq)])

    @pl.when(i == p_tiles)
    def _():
        run(_Mixer.sample_band, lambda m: [None] * nq)


def _attention(a, b, *, layer, p_tiles):
    rows = a[0].shape[0]
    nq = TM // CHUNK
    cur = lambda width: pl.BlockSpec((TM, width), lambda i: (i, 0))
    in_specs, args, scratch = [], [], []
    for q, k, v, cache_k, cache_v, extra, prev in (a, b):
        kw = k.shape[1]
        hist = prev * CHUNK
        win = hist + CHUNK
        assert TM % hist == 0 and rows == (p_tiles + 1) * TM and cache_k.shape[1:] == (nq * hist, kw)
        prev_spec = pl.BlockSpec((hist, kw), lambda i, r=TM // hist: (jnp.maximum(i * r - 1, 0), 0))
        in_specs += [cur(ATT_W), prev_spec, cur(kw), prev_spec, cur(kw),
                     _layer_resident(cache_k, layer), _layer_resident(cache_v, layer), _layer_resident(extra, layer)]
        args += [q, k, k, v, v, cache_k, cache_v, extra]
        scratch += [pltpu.VMEM((hist + TM, kw), BF16), pltpu.VMEM((hist + TM, kw), BF16),
                    pltpu.VMEM((nq, N_SLABS * PAIR, win), F32),
                    pltpu.VMEM((nq, N_SLABS * PAIR, win), BF16),
                    pltpu.VMEM((nq, N_SLABS * PAIR, 1), F32)]
    return pl.pallas_call(
        functools.partial(_attn_kernel, p_tiles=p_tiles),
        grid=(p_tiles + 1,),
        in_specs=in_specs,
        out_specs=[cur(ATT_W), cur(ATT_W)],
        out_shape=[jax.ShapeDtypeStruct((rows, ATT_W), BF16)] * 2,
        scratch_shapes=scratch,
        compiler_params=_params("arbitrary"),
        name="band_attn",
    )(*args)


HGRN_SAFE_LOG_DECAY = -60.0
HGRN_GROUP = 128


def _hgrn_kernel(c_ref, lbl_ref, cn_ref, s0_ref, o_ref, sfp_ref, sfs_ref,
                 st_ref, qd_ref, kd_ref, v_ref, dec_ref, oi_ref, oin_ref, u_ref, sall_ref,
                 lfp_ref, q_ref, k_ref, kinv_ref, sc_ref, vt_ref, *, layer, depth, p_tiles, n_dec, dec_seq):
    tm = c_ref.shape[0]
    i = pl.program_id(0)
    is_sample = i == p_tiles

    @pl.when(i == 0)
    def _():
        st_ref[...] = jnp.zeros_like(st_ref)

    logit_rows = [lbl_ref[r:r + 1, :] for r in range(depth)]
    mx = functools.reduce(jnp.maximum, logit_rows)
    ex = [jnp.exp(r - mx) for r in logit_rows]
    den = functools.reduce(jnp.add, ex)
    sm = [e / den for e in ex]
    lb_all = functools.reduce(jnp.add, sm[:layer + 1]) - sm[0]
    log_lb = jnp.log(lb_all)
    log_1mlb = jnp.log1p(-lb_all)

    def gates(rows, hd):
        lanes = slice(hd * C_DIM, (hd + 1) * C_DIM)
        z = c_ref[rows, lanes].astype(F32)
        vv = c_ref[rows, C_W + hd * C_DIM:C_W + (hd + 1) * C_DIM].astype(F32)
        qp = c_ref[rows, 2 * C_W + hd * C_DIM:2 * C_W + (hd + 1) * C_DIM].astype(F32)
        q = qp * jax.nn.sigmoid(qp)
        log_sig = -(jnp.maximum(-z, 0.0) + jnp.log(1.0 + jnp.exp(-jnp.abs(z))))
        a = log_lb[:, lanes]
        b = log_1mlb[:, lanes] + log_sig
        lf = jnp.maximum(a, b) + jnp.log(1.0 + jnp.exp(-jnp.abs(a - b)))
        return q, 1.0 - jnp.exp(lf), lf, vv

    grow = lax.broadcasted_iota(jnp.int32, (HGRN_GROUP, HGRN_GROUP), 0)
    gcol = lax.broadcasted_iota(jnp.int32, (HGRN_GROUP, HGRN_GROUP), 1)
    same_block = grow // REC_BLOCK == gcol // REC_BLOCK
    upto = jnp.logical_and(same_block, gcol <= grow)
    after = jnp.logical_and(same_block, gcol > grow)
    sum_upto = jnp.where(upto, 1.0, 0.0).astype(BF16)
    sum_after = jnp.where(after, 1.0, 0.0).astype(BF16)

    def block_sums(ones, parts):
        r = jnp.dot(ones, parts, preferred_element_type=F32)
        return r[:, :C_DIM] + r[:, C_DIM:2 * C_DIM] + r[:, 2 * C_DIM:]

    n_groups = tm // HGRN_GROUP
    items = [(hd, slice(g * HGRN_GROUP, (g + 1) * HGRN_GROUP))
             for hd in range(C_HEADS) for g in range(n_groups)]
    for hd, rows in items:
        q, kk, lf, vv = gates(rows, hd)
        hi = lf.astype(BF16)
        rest = lf - hi.astype(F32)
        mid = rest.astype(BF16)
        lfp_ref[hd, rows, :] = jnp.concatenate([hi, mid, (rest - mid.astype(F32)).astype(BF16)], axis=1)
        q_ref[hd, rows, :] = q
        k_ref[hd, rows, :] = kk
        v_ref[hd, rows, :] = vv.astype(BF16)
        vt_ref[hd, :, rows] = vv.T.astype(BF16)
    cum_min = jnp.zeros((HGRN_GROUP, C_DIM), F32)
    for hd, rows in items:
        parts = lfp_ref[hd, rows, :]
        cum = block_sums(sum_upto, parts)
        rem = block_sums(sum_after, parts)
        kk = k_ref[hd, rows, :]
        qd_ref[hd, rows, :] = (q_ref[hd, rows, :] * jnp.exp(cum)).astype(BF16)
        kd_ref[hd, rows, :] = (kk * jnp.exp(rem)).astype(BF16)
        kinv_ref[hd, rows, :] = (kk * jnp.exp(-cum)).astype(BF16)
        dec_ref[hd, rows, :] = jnp.exp(cum + rem)
        cum_min = jnp.minimum(cum_min, cum)
    for hd, rows in items:
        sc = lax.dot_general(qd_ref[hd, rows, :], kinv_ref[hd, rows, :], (((1,), (1,)), ((), ())),
                             preferred_element_type=F32)
        sc_ref[hd, rows, :] = jnp.where(upto, sc, 0.0).astype(BF16)
    for hd, rows in items:
        oin_ref[hd, rows, :] = jnp.dot(sc_ref[hd, rows, :], v_ref[hd, rows, :], preferred_element_type=F32)

    blocks = HGRN_GROUP // REC_BLOCK
    in_block = (lax.broadcasted_iota(jnp.int32, (HGRN_GROUP, blocks * C_DIM), 0) // REC_BLOCK
                == lax.broadcasted_iota(jnp.int32, (HGRN_GROUP, blocks * C_DIM), 1) // C_DIM)

    def block_diagonal(a):
        return jnp.where(in_block, jnp.concatenate([a] * blocks, axis=1), jnp.zeros((), a.dtype))

    for gi, (hd, rows) in enumerate(items):
        u_ref[hd, gi % n_groups] = jnp.dot(vt_ref[hd, :, rows], block_diagonal(kd_ref[hd, rows, :]),
                                           preferred_element_type=F32)

    per_stream = dec_seq // REC_BLOCK
    for hd in range(C_HEADS):
        st = st_ref[hd]
        for g in range(n_groups):
            rows = slice(g * HGRN_GROUP, (g + 1) * HGRN_GROUP)
            for j in range(blocks):
                n = g * blocks + j
                if n % per_stream == 0:
                    st = jnp.where(is_sample, s0_ref[n // per_stream, hd], st)
                lanes = slice(j * C_DIM, (j + 1) * C_DIM)
                sall_ref[hd, g, :, lanes] = st.astype(BF16)
                dec = dec_ref[hd, n * REC_BLOCK:n * REC_BLOCK + 8, :]
                st = st * jnp.concatenate([dec] * (C_DIM // 8), axis=0) + u_ref[hd, g, :, lanes]
                if (n + 1) % per_stream == 0:
                    sfs_ref[n // per_stream, hd] = st
            o_t = lax.dot_general(sall_ref[hd, g], block_diagonal(qd_ref[hd, rows, :]),
                                  (((1,), (1,)), ((), ())), preferred_element_type=F32)
            oi_ref[hd, rows, :] = o_t.T
        st_ref[hd] = st

    def finish():
        for hd, rows in items:
            lanes = slice(hd * C_DIM, (hd + 1) * C_DIM)
            o = oin_ref[hd, rows, :] + oi_ref[hd, rows, :]
            o = o * lax.rsqrt(jnp.mean(o * o, axis=-1, keepdims=True) + NORM_EPS) * cn_ref[:, lanes]
            og = c_ref[rows, 3 * C_W + hd * C_DIM:3 * C_W + (hd + 1) * C_DIM].astype(F32)
            o_ref[rows, lanes] = (o * (og * jax.nn.sigmoid(og))).astype(o_ref.dtype)

    finish()

    @pl.when(i == p_tiles - 1)
    def _():
        sfp_ref[0] = st_ref[...]

    @pl.when(jnp.min(cum_min) <= HGRN_SAFE_LOG_DECAY)
    def _():
        pos = lax.broadcasted_iota(jnp.int32, (tm, C_DIM), 0) % REC_BLOCK
        for hd in range(C_HEADS):
            q, kk, lf, vv = gates(slice(None), hd)
            cum = lf
            for sft in (1, 2, 4, 8):
                cum = cum + jnp.where(pos >= sft, pltpu.roll(cum, sft, 0), 0.0)
            acc = jnp.zeros((tm, C_DIM), F32)
            for d in range(REC_BLOCK):
                kr = kk if d == 0 else pltpu.roll(kk, d, 0)
                cr = cum if d == 0 else pltpu.roll(cum, d, 0)
                vr = vv if d == 0 else pltpu.roll(vv, d, 0)
                w = q * kr * jnp.exp(jnp.minimum(cum - cr, 0.0))
                score = jnp.sum(w, axis=-1, keepdims=True)
                acc = acc + jnp.where(pos >= d, score, 0.0) * vr
            oin_ref[hd] = acc
        finish()


def _hgrn(c, lb_logits, c_norm, s0, *, layer, p_tiles, dec_seq):
    rows = c.shape[0]
    depth, n_dec = s0.shape[:2]
    assert rows == (p_tiles + 1) * TM and n_dec * dec_seq == TM and dec_seq % REC_BLOCK == 0
    assert TM % HGRN_GROUP == 0 and dec_seq * 2 == HGRN_GROUP
    per_group = (C_HEADS, TM // HGRN_GROUP, C_DIM, HGRN_GROUP // REC_BLOCK * C_DIM)
    state = (C_HEADS, C_DIM, C_DIM)
    kern = functools.partial(_hgrn_kernel, layer=layer, depth=depth, p_tiles=p_tiles, n_dec=n_dec, dec_seq=dec_seq)
    return pl.pallas_call(
        kern,
        grid=(p_tiles + 1,),
        in_specs=[pl.BlockSpec((TM, 4 * C_W), lambda i: (i, 0)), _resident(lb_logits.shape),
                  _layer_resident(c_norm, layer), _layer_resident(s0, layer)],
        out_specs=[pl.BlockSpec((TM, C_W), lambda i: (i, 0)),
                   pl.BlockSpec((1,) + state, lambda i: (0, 0, 0, 0)),
                   pl.BlockSpec((n_dec,) + state, lambda i: (0, 0, 0, 0))],
        out_shape=[jax.ShapeDtypeStruct((rows, C_W), BF16),
                   jax.ShapeDtypeStruct((1,) + state, F32),
                   jax.ShapeDtypeStruct((n_dec,) + state, F32)],
        scratch_shapes=[pltpu.VMEM(state, F32),
                        pltpu.VMEM((C_HEADS, TM, C_DIM), BF16),
                        pltpu.VMEM((C_HEADS, TM, C_DIM), BF16),
                        pltpu.VMEM((C_HEADS, TM, C_DIM), BF16),
                        pltpu.VMEM((C_HEADS, TM, C_DIM), F32),
                        pltpu.VMEM((C_HEADS, TM, C_DIM), F32),
                        pltpu.VMEM((C_HEADS, TM, C_DIM), F32),
                        pltpu.VMEM(per_group, F32),
                        pltpu.VMEM(per_group, BF16),
                        pltpu.VMEM((C_HEADS, TM, 3 * C_DIM), BF16),
                        pltpu.VMEM((C_HEADS, TM, C_DIM), F32),
                        pltpu.VMEM((C_HEADS, TM, C_DIM), F32),
                        pltpu.VMEM((C_HEADS, TM, C_DIM), BF16),
                        pltpu.VMEM((C_HEADS, TM, C_DIM), BF16),
                        pltpu.VMEM((C_HEADS, C_DIM, TM), BF16)],
        compiler_params=_params("arbitrary"),
        name="hgrn2",
    )(c, lb_logits, c_norm, s0)


FFN_COLS = 256
SUBLANES = 8


def _ffn_kernel(*refs, n_x, p_tiles, n_seg, seg_len, d_ff, final_norm, split_out):
    refs = list(refs)
    x_refs = refs[:n_x]
    (oa_ref, ob_ref, oc_ref, ga_ref, gb_ref, gc_ref, wa_ref, wb_ref, wc_ref, wo_ref,
     g2_ref, wup_ref, wconv_ref, bconv_ref, wdown_ref, state_ref) = refs[n_x:n_x + 16]
    refs = refs[n_x + 16:]
    gfin_ref = refs.pop(0) if final_norm else None
    y_refs = [refs.pop(0) for _ in range(2 if split_out else 1)]
    newp_ref, news_ref, act_ref, carry_ref = refs
    cw = FFN_COLS
    n_chunks = d_ff // cw
    tm = n_seg * seg_len
    groups_per_seg = seg_len // SUBLANES
    i = pl.program_id(0)
    is_sample = i == p_tiles

    @pl.when(i == 0)
    def _():
        carry_ref[...] = jnp.zeros_like(carry_ref)

    def branch(o_ref, w_ref, g_ref):
        return jax.nn.sigmoid(g_ref[...].astype(F32)) * jnp.dot(o_ref[...], w_ref[...], preferred_element_type=F32)

    mix = branch(oa_ref, wa_ref, ga_ref) + branch(ob_ref, wb_ref, gb_ref) + branch(oc_ref, wc_ref, gc_ref)
    x = _stacked_rows(x_refs, p_tiles) + jnp.dot(mix.astype(BF16), wo_ref[...], preferred_element_type=F32)
    h = _rmsnorm(x, g2_ref[...]).astype(BF16)
    sub = lax.broadcasted_iota(jnp.int32, (SUBLANES, cw), 0)

    def up_proj(c):
        return [jnp.dot(h, wup_ref[:, col0:col0 + cw], preferred_element_type=F32)
                for col0 in (c * cw, d_ff + c * cw)]

    def conv(up, col0):
        cols = slice(col0, col0 + cw)
        w0 = wconv_ref[0:1, cols]
        w1 = wconv_ref[1:2, cols]
        w2 = wconv_ref[2:3, cols]
        bias = bconv_ref[:, cols]
        rolled = lambda g: (pltpu.roll(g, 1, 0), pltpu.roll(g, 2, 0))
        prev = carry_ref[:, cols]
        prev_rolled = None
        out = []
        for gi in range(tm // SUBLANES):
            grp = up[gi * SUBLANES:(gi + 1) * SUBLANES]
            if gi % groups_per_seg == 0:
                sg = gi // groups_per_seg
                st = state_ref[sg, :, cols]
                state_grp = jnp.where(sub == SUBLANES - 2, st[0:1], st[1:2])
                prev_rolled = rolled(jnp.where(is_sample, state_grp, prev))
            cur_rolled = rolled(grp)
            back1 = jnp.where(sub < 1, prev_rolled[0], cur_rolled[0])
            back2 = jnp.where(sub < 2, prev_rolled[1], cur_rolled[1])
            out.append(bias + w0 * back2 + w1 * back1 + w2 * grp)
            prev, prev_rolled = grp, cur_rolled
            if (gi + 1) % groups_per_seg == 0:
                news_ref[gi // groups_per_seg, :, cols] = grp[SUBLANES - 2:]
        carry_ref[:, cols] = prev
        return jnp.concatenate(out, axis=0)

    ups = up_proj(0)
    for c in range(n_chunks):
        nxt = up_proj(c + 1) if c + 1 < n_chunks else None
        u = conv(ups[0], c * cw)
        g = conv(ups[1], d_ff + c * cw)
        act_ref[:, c * cw:(c + 1) * cw] = (u * jax.nn.gelu(g)).astype(BF16)
        ups = nxt
    y = x + jnp.dot(act_ref[...], wdown_ref[...], preferred_element_type=F32)
    if final_norm:
        y = _rmsnorm(y, gfin_ref[...])

    if split_out:
        @pl.when(i < p_tiles)
        def _():
            y_refs[0][...] = y

        @pl.when(is_sample)
        def _():
            y_refs[1][...] = y
    else:
        y_refs[0][...] = y

    @pl.when(i == p_tiles - 1)
    def _():
        newp_ref[0] = carry_ref[SUBLANES - 2:SUBLANES, :]


def _mix_ffn(xs, oa, ob, oc, gates, mix_w, ffn_w, state, g_final, *, layer, p_tiles, dec_seq, split_out):
    rows = sum(x.shape[0] for x in xs)
    d_model = xs[0].shape[1]
    d_ff = ffn_w[4].shape[1]
    n_dec = state.shape[1]
    assert d_ff % FFN_COLS == 0 and n_dec * dec_seq == TM and rows == (p_tiles + 1) * TM
    assert dec_seq % SUBLANES == 0
    final_norm = g_final is not None
    row_block = lambda width, j=0: pl.BlockSpec((TM, width), lambda i: (i, j))
    x_specs, first_tiles = _stacked_specs(xs, TM)
    assert len(xs) == 1 or first_tiles == p_tiles
    resident = list(mix_w) + list(ffn_w) + [state]
    args = list(xs) + [oa, ob, oc, gates, gates, gates] + resident
    in_specs = (x_specs + [row_block(ATT_W), row_block(ATT_W), row_block(C_W),
                           row_block(d_model, 0), row_block(d_model, 1), row_block(d_model, 2)]
                + [_layer_resident(a, layer) for a in resident])
    if final_norm:
        in_specs.append(_resident(g_final.shape))
        args.append(g_final)
    if split_out:
        y_specs = [pl.BlockSpec((TM, d_model), lambda i: (jnp.minimum(i, p_tiles - 1), 0)),
                   pl.BlockSpec((TM, d_model), lambda i: (0, 0))]
        y_shapes = [jax.ShapeDtypeStruct((p_tiles * TM, d_model), F32), jax.ShapeDtypeStruct((TM, d_model), F32)]
    else:
        y_specs = [pl.BlockSpec((TM, d_model), lambda i: (i, 0))]
        y_shapes = [jax.ShapeDtypeStruct((rows, d_model), F32)]
    conv_state = lambda n: (n, CONV_W - 1, 2 * d_ff)
    kern = functools.partial(_ffn_kernel, n_x=len(xs), p_tiles=p_tiles, n_seg=n_dec, seg_len=dec_seq, d_ff=d_ff,
                             final_norm=final_norm, split_out=split_out)
    return pl.pallas_call(
        kern,
        grid=(p_tiles + 1,),
        in_specs=in_specs,
        out_specs=y_specs + [pl.BlockSpec(conv_state(1), lambda i: (0, 0, 0)),
                             pl.BlockSpec(conv_state(n_dec), lambda i: (0, 0, 0))],
        out_shape=y_shapes + [jax.ShapeDtypeStruct(conv_state(1), F32),
                              jax.ShapeDtypeStruct(conv_state(n_dec), F32)],
        scratch_shapes=[pltpu.VMEM((TM, d_ff), BF16),
                        pltpu.VMEM((SUBLANES, 2 * d_ff), F32)],
        compiler_params=_params("arbitrary"),
        name="convffn",
    )(*args)


def _rope_tables(pos):
    half = ROPE_DIM // 2
    inv_freq = ROPE_THETA ** (-jnp.arange(0, ROPE_DIM, 2, dtype=F32) / ROPE_DIM)
    ang = pos.astype(F32)[:, None] * inv_freq[None, :]
    rotated = (jnp.arange(LANES) % HEAD_DIM < ROPE_DIM)[None, :]
    tile = lambda t: jnp.tile(t, (1, LANES // half))
    return jnp.where(rotated, tile(jnp.cos(ang)), 1.0), jnp.where(rotated, tile(jnp.sin(ang)), 0.0)


def _permute_a_heads(a, axis, unit=HEAD_DIM):
    shape = a.shape
    assert shape[axis] == A_Q_HEADS * unit
    a = a.reshape(shape[:axis] + (A_KV_HEADS, A_GROUP, unit) + shape[axis + 1:])
    return jnp.swapaxes(a, axis, axis + 1).reshape(shape)


def _prep_wq(w_in):
    scale = HEAD_DIM ** -0.5 * LOG2_E
    b0 = ATT_W + 2 * A_KV_W
    return (jnp.concatenate([_permute_a_heads(w_in[..., :ATT_W], w_in.ndim - 1), w_in[..., b0:b0 + ATT_W]],
                            axis=-1) * scale).astype(BF16)


def _rel_bias_pairs(table):
    assert CHUNK - 1 <= REL_CLIP
    reach = B_PREV_CHUNKS * CHUNK
    width = reach + CHUNK
    n_clipped = reach + CHUNK - 1 - REL_CLIP
    table = table.astype(F32).reshape(-1, 2 * REL_CLIP + 1) * LOG2_E
    rows = table.shape[0]
    diag = jnp.concatenate([jnp.broadcast_to(table[:, 2 * REL_CLIP:], (rows, n_clipped)),
                            table[:, REL_CLIP - (CHUNK - 1):][:, ::-1]], axis=1)
    period = width + CHUNK
    ring = jnp.concatenate([diag[:, CHUNK - 1:], jnp.zeros((rows, 2), F32), diag[:, :CHUNK - 1]], axis=1)
    assert ring.shape[1] == period + 1
    bias = jnp.tile(ring, (1, CHUNK))[:, :CHUNK * period].reshape(rows, CHUNK, period)[:, :, :width]
    return bias.reshape(rows // B_HEADS, B_HEADS // 2, PAIR, width)


def _sink_pairs(sinks):
    depth = sinks.shape[0]
    s = _permute_a_heads(sinks.astype(F32) * LOG2_E, 1, unit=1)
    return jnp.broadcast_to(s[:, :, None], (depth, A_Q_HEADS, CHUNK)).reshape(depth, N_SLABS, PAIR, 1)


def kernel(x_prompt, x_sample, cache_a_k, cache_a_v, cache_b_k, cache_b_v, state_c, state_ffn_conv,
           norm1, w_in, a_sinks, b_rel_bias, c_lb_logits, c_norm, w_branch_a, w_branch_b, w_branch_c,
           w_out, norm2, w_up, w_conv, b_conv, w_down, norm_final):
    batch, seq, d_model = x_prompt.shape
    dec_batch, dec_seq, _ = x_sample.shape
    depth = w_in.shape[0]
    d_ff = w_down.shape[1]
    ra = cache_a_k.shape[2]
    rb = cache_b_k.shape[2]
    n_prompt = batch * seq
    n_sample = dec_batch * dec_seq
    assert batch == 1 and dec_seq == CHUNK and seq % TM == 0 and n_sample == TM
    assert ra == A_PREV_CHUNKS * CHUNK and rb == B_PREV_CHUNKS * CHUNK
    p_tiles = n_prompt // TM

    xs = [x_prompt.reshape(n_prompt, d_model), x_sample.reshape(n_sample, d_model)]
    pos = jnp.concatenate([jnp.arange(seq, dtype=jnp.int32),
                           jnp.tile(PAST_LEN + jnp.arange(dec_seq, dtype=jnp.int32), dec_batch)])
    cos, sin = _rope_tables(pos)

    w_in_k = w_in.astype(BF16)
    wq_k = _prep_wq(w_in)
    wa_k = _permute_a_heads(w_branch_a, 1).astype(BF16)
    wb_k, wc_k, wo_k = w_branch_b.astype(BF16), w_branch_c.astype(BF16), w_out.astype(BF16)
    w_up_k, w_down_k = w_up.astype(BF16), w_down.astype(BF16)
    norm1_k = norm1.reshape(depth, 1, d_model)
    norm2_k = norm2.reshape(depth, 1, d_model)
    b_conv_k = b_conv.reshape(depth, 1, 2 * d_ff)
    c_norm_k = c_norm.reshape(depth, 1, C_W).astype(F32)
    lb_logits = c_lb_logits.astype(F32)
    ca_k = cache_a_k.reshape(depth, dec_batch * ra, A_KV_W).astype(BF16)
    ca_v = cache_a_v.reshape(depth, dec_batch * ra, A_KV_W).astype(BF16)
    cb_k = cache_b_k.reshape(depth, dec_batch * rb, ATT_W).astype(BF16)
    cb_v = cache_b_v.reshape(depth, dec_batch * rb, ATT_W).astype(BF16)
    state_c_k = jnp.swapaxes(state_c.astype(F32), -1, -2)
    sinks_k = _sink_pairs(a_sinks)
    bias_k = _rel_bias_pairs(b_rel_bias)

    new_kv = []
    c_states_p, c_states_s, convs_p, convs_s = [], [], [], []
    y_prompt = y_sample = None
    for l in range(depth):
        aq, ak, av, bq, bk, bv, c, gates = _inproj(xs, norm1_k, w_in_k, wq_k, cos, sin, layer=l, tm=TM)
        oa, ob = _attention((aq, ak, av, ca_k, ca_v, sinks_k, A_PREV_CHUNKS),
                            (bq, bk, bv, cb_k, cb_v, bias_k, B_PREV_CHUNKS),
                            layer=l, p_tiles=p_tiles)
        oc, c_state_p, c_state_s = _hgrn(c, lb_logits, c_norm_k, state_c_k, layer=l, p_tiles=p_tiles,
                                         dec_seq=dec_seq)
        last = l == depth - 1
        outs = _mix_ffn(xs, oa, ob, oc, gates, (wa_k, wb_k, wc_k, wo_k),
                        (norm2_k, w_up_k, w_conv, b_conv_k, w_down_k), state_ffn_conv,
                        norm_final.reshape(1, d_model) if last else None,
                        layer=l, p_tiles=p_tiles, dec_seq=dec_seq, split_out=last)
        if last:
            y_prompt, y_sample, conv_p, conv_s = outs
        else:
            x2, conv_p, conv_s = outs
            xs = [x2]
        new_kv.append((ak, av, bk, bv))
        c_states_p.append(c_state_p)
        c_states_s.append(c_state_s)
        convs_p.append(conv_p)
        convs_s.append(conv_s)

    def new_cache(idx, cache, reach, n_heads):
        prompt = jnp.stack([t[idx][n_prompt - reach:n_prompt] for t in new_kv]).astype(F32)
        fresh = jnp.stack([t[idx][n_prompt:] for t in new_kv]).astype(F32)
        return (prompt.reshape(depth, batch, reach, n_heads, HEAD_DIM),
                jnp.concatenate([cache[:, :, dec_seq:],
                                 fresh.reshape(depth, dec_batch, dec_seq, n_heads, HEAD_DIM)], axis=2))

    a_k_p, a_k_s = new_cache(0, cache_a_k, ra, A_KV_HEADS)
    a_v_p, a_v_s = new_cache(1, cache_a_v, ra, A_KV_HEADS)
    b_k_p, b_k_s = new_cache(2, cache_b_k, rb, B_HEADS)
    b_v_p, b_v_s = new_cache(3, cache_b_v, rb, B_HEADS)
    return (y_prompt.reshape(batch, seq, d_model), y_sample.reshape(dec_batch, dec_seq, d_model),
            a_k_p, a_v_p, b_k_p, b_v_p, jnp.swapaxes(jnp.stack(c_states_p), -1, -2), jnp.stack(convs_p),
            a_k_s, a_v_s, b_k_s, b_v_s, jnp.swapaxes(jnp.stack(c_states_s), -1, -2), jnp.stack(convs_s))
```
